```python
import jax, jax.numpy as jnp
from jax import lax
import numpy as np

D_MODEL = 2048
BATCH = 4
SEQ = 8192
DEPTH = 2
DEC_BATCH = 16
DEC_SEQ = 16
PAST_LEN = 1024

CHUNK = 64
N_HEADS = 32
HEAD_DIM = D_MODEL // N_HEADS
N_LEFT_CHUNKS = 8
ATT_REACH = N_LEFT_CHUNKS * CHUNK
BAND = ATT_REACH + CHUNK
REL_CLIP = 256
D_RNN = D_MODEL
N_RNN_BLOCKS = 8
RNN_BLOCK = D_RNN // N_RNN_BLOCKS
CONV_W = 4
RG_C = 8.0
D_FF = 5632
N_EXPERTS = 8
TOP_K = 2
MOE_FF = 5632
MOE_BLOCK = 512
N_ATT_LAYERS = (DEPTH + 1) // 2
N_RNN_LAYERS = DEPTH // 2
N_DENSE_FFN = (DEPTH + 1) // 2
N_MOE_FFN = DEPTH // 2
ALPHA = (2.0 * DEPTH) ** 0.25
BETA = (8.0 * DEPTH) ** -0.25
LN_EPS = 1e-5
NEG_INF = -1e30

kernel_name = 'hybrid_chunk_attn_rglru_stream_step'


def layer_norm(x, g, b):
    xf = x.astype(jnp.float32)
    mu = jnp.mean(xf, axis=-1, keepdims=True)
    var = jnp.mean(jnp.square(xf - mu), axis=-1, keepdims=True)
    return ((xf - mu) * lax.rsqrt(var + LN_EPS) * g.astype(jnp.float32) + b.astype(jnp.float32)).astype(x.dtype)


def qkv_heads(x, w_in):
    b, t, _ = x.shape
    qkv = jnp.einsum('btd,de->bte', x, w_in).reshape(b, t, 3, N_HEADS, HEAD_DIM)
    return qkv[:, :, 0], qkv[:, :, 1], qkv[:, :, 2]


def band_attention(q, k, v, q_pos, k_pos, rel_bias):
    s = jnp.einsum('bqhd,bkhd->bhqk', q.astype(jnp.float32), k.astype(jnp.float32)) * (HEAD_DIM ** -0.5)
    rel = jnp.clip(q_pos[:, None] - k_pos[None, :], -REL_CLIP, REL_CLIP) + REL_CLIP
    s = s + rel_bias.astype(jnp.float32)[:, rel][None]
    dc = q_pos[:, None] // CHUNK - k_pos[None, :] // CHUNK
    mask = (k_pos[None, :] >= 0) & (dc >= 0) & (dc <= N_LEFT_CHUNKS)
    s = jnp.where(mask[None, None], s, NEG_INF)
    p = jax.nn.softmax(s, axis=-1)
    return jnp.einsum('bhqk,bkhd->bqhd', p.astype(v.dtype), v)


def chunk_attention_prompt(q, k, v, rel_bias):
    b, s = q.shape[:2]
    pad = ((0, 0), (ATT_REACH, 0), (0, 0), (0, 0))
    k_pad = jnp.pad(k, pad)
    v_pad = jnp.pad(v, pad)
    offs_q = jnp.arange(CHUNK, dtype=jnp.int32)
    offs_k = jnp.arange(BAND, dtype=jnp.int32) - ATT_REACH

    def one_chunk(c):
        start = c * CHUNK
        qc = lax.dynamic_slice_in_dim(q, start, CHUNK, axis=1)
        kc = lax.dynamic_slice_in_dim(k_pad, start, BAND, axis=1)
        vc = lax.dynamic_slice_in_dim(v_pad, start, BAND, axis=1)
        return band_attention(qc, kc, vc, start + offs_q, start + offs_k, rel_bias)

    out = lax.map(one_chunk, jnp.arange(s // CHUNK, dtype=jnp.int32))
    return jnp.moveaxis(out, 0, 1).reshape(b, s, N_HEADS, HEAD_DIM)


def chunk_attention_sample(q, k_new, v_new, cache_k, cache_v, rel_bias):
    n_cache = cache_k.shape[1]
    t = q.shape[1]
    k = jnp.concatenate([cache_k.astype(k_new.dtype), k_new], axis=1)
    v = jnp.concatenate([cache_v.astype(v_new.dtype), v_new], axis=1)
    q_pos = PAST_LEN + jnp.arange(t, dtype=jnp.int32)
    k_pos = PAST_LEN - n_cache + jnp.arange(n_cache + t, dtype=jnp.int32)
    return band_attention(q, k, v, q_pos, k_pos, rel_bias)


def _lin_combine(left, right):
    a_l, b_l = left
    a_r, b_r = right
    return a_l * a_r, a_r * b_l + b_r


def rglru_mixer(x, conv_state, h0, w_in, conv_w, conv_b, w_a, b_a, w_x, b_x, lam, w_out):
    b, t, _ = x.shape
    u, gate = jnp.split(jnp.einsum('btd,de->bte', x, w_in), 2, axis=-1)
    u_ext = jnp.concatenate([conv_state.astype(u.dtype), u], axis=1)
    uc = conv_b
    for j in range(CONV_W):
        uc = uc + u_ext[:, j:j + t] * conv_w[j]
    ub = uc.reshape(b, t, N_RNN_BLOCKS, RNN_BLOCK)
    r = jax.nn.sigmoid((jnp.einsum('btnc,nce->btne', ub, w_a) + b_a).astype(jnp.float32)).reshape(b, t, D_RNN)
    i = jax.nn.sigmoid((jnp.einsum('btnc,nce->btne', ub, w_x) + b_x).astype(jnp.float32)).reshape(b, t, D_RNN)
    log_a = -RG_C * r * jax.nn.softplus(-lam.astype(jnp.float32))
    a = jnp.exp(log_a)
    bterm = jnp.sqrt(-jnp.expm1(2.0 * log_a)) * (i * uc.astype(jnp.float32))
    bterm = bterm.at[:, 0].add(a[:, 0] * h0.astype(jnp.float32))
    _, h = lax.associative_scan(_lin_combine, (a, bterm), axis=1)
    y = jnp.einsum('bte,ed->btd', (jax.nn.gelu(gate.astype(jnp.float32)) * h).astype(x.dtype), w_out)
    return y, u_ext[:, -(CONV_W - 1):], h[:, -1].astype(x.dtype)


def swiglu(x, w_in, w_out):
    g, u = jnp.split(x @ w_in, 2, axis=-1)
    return (jax.nn.silu(g) * u) @ w_out


def moe_swiglu(x, w_router, w_in, w_out):
    n_tok = x.shape[0]
    logits = jnp.dot(x.astype(jnp.float32), w_router.astype(jnp.float32))
    top_logit, top_idx = lax.top_k(logits, TOP_K)
    gates = jax.nn.softmax(top_logit, axis=-1).astype(x.dtype)
    n_assign = n_tok * TOP_K
    flat_e = top_idx.reshape(-1)
    flat_tok = jnp.arange(n_assign, dtype=jnp.int32) // TOP_K
    flat_gate = gates.reshape(-1)
    order = jnp.argsort(flat_e)
    e_sorted = flat_e[order]
    counts = jnp.bincount(flat_e, length=N_EXPERTS)
    padded = (counts + MOE_BLOCK - 1) // MOE_BLOCK * MOE_BLOCK
    start = jnp.cumsum(counts) - counts
    ends_pad = jnp.cumsum(padded)
    start_pad = ends_pad - padded
    dest = start_pad[e_sorted] + jnp.arange(n_assign, dtype=jnp.int32) - start[e_sorted]
    n_blocks = -(-n_assign // MOE_BLOCK) + N_EXPERTS
    n_rows = n_blocks * MOE_BLOCK
    row_tok = jnp.full((n_rows,), n_tok, jnp.int32).at[dest].set(flat_tok[order])
    row_gate = jnp.zeros((n_rows,), x.dtype).at[dest].set(flat_gate[order])
    block_start = jnp.arange(n_blocks, dtype=jnp.int32) * MOE_BLOCK
    block_e = jnp.minimum(jnp.searchsorted(ends_pad, block_start, side='right'), N_EXPERTS - 1)
    x_pad = jnp.concatenate([x, jnp.zeros((1, x.shape[1]), x.dtype)], axis=0)
    xb = x_pad[row_tok].reshape(n_blocks, MOE_BLOCK, x.shape[1])

    def expert_block(args):
        xblk, e = args
        return swiglu(xblk, w_in[e], w_out[e])

    yb = lax.map(expert_block, (xb, block_e)).reshape(n_rows, x.shape[1])
    y = jnp.zeros((n_tok + 1, x.shape[1]), x.dtype).at[row_tok].add(yb * row_gate[:, None])
    return y[:n_tok]


def setup_inputs(seed: int = 0) -> dict:
    key = jax.random.key(seed)
    ks = iter(jax.random.split(key, 32))

    def nrm(shape, scale=1.0):
        return jax.random.normal(next(ks), shape, jnp.float32) * scale

    att_cache = min(ATT_REACH, PAST_LEN)
    na, nr = N_ATT_LAYERS, N_RNN_LAYERS
    a0 = jax.random.uniform(next(ks), (nr, D_RNN), jnp.float32, 0.9, 0.999)
    sig = a0 ** (1.0 / RG_C)
    rg_lambda = jnp.log(sig) - jnp.log1p(-sig)
    return {
        'x_prompt': nrm((BATCH, SEQ, D_MODEL)),
        'x_sample': nrm((DEC_BATCH, DEC_SEQ, D_MODEL)),
        'cache_k': nrm((na, DEC_BATCH, att_cache, N_HEADS, HEAD_DIM)),
        'cache_v': nrm((na, DEC_BATCH, att_cache, N_HEADS, HEAD_DIM)),
        'state_conv': nrm((nr, DEC_BATCH, CONV_W - 1, D_RNN)),
        'state_h': nrm((nr, DEC_BATCH, D_RNN), 0.5),
        'ln_g': 1.0 + nrm((DEPTH, 2, D_MODEL), 0.02),
        'ln_b': nrm((DEPTH, 2, D_MODEL), 0.02),
        'w_attn_in': nrm((na, D_MODEL, 3 * D_MODEL), D_MODEL ** -0.5),
        'rel_bias': nrm((na, N_HEADS, 2 * REL_CLIP + 1), 0.1),
        'w_attn_out': nrm((na, D_MODEL, D_MODEL), BETA * D_MODEL ** -0.5),
        'w_rnn_in': nrm((nr, D_MODEL, 2 * D_RNN), D_MODEL ** -0.5),
        'conv_w': nrm((nr, CONV_W, D_RNN), CONV_W ** -0.5),
        'conv_b': nrm((nr, D_RNN), 0.02),
        'w_rg_a': nrm((nr, N_RNN_BLOCKS, RNN_BLOCK, RNN_BLOCK), RNN_BLOCK ** -0.5),
        'b_rg_a': nrm((nr, N_RNN_BLOCKS, RNN_BLOCK), 0.02),
        'w_rg_x': nrm((nr, N_RNN_BLOCKS, RNN_BLOCK, RNN_BLOCK), RNN_BLOCK ** -0.5),
        'b_rg_x': nrm((nr, N_RNN_BLOCKS, RNN_BLOCK), 0.02),
        'rg_lambda': rg_lambda,
        'w_rnn_out': nrm((nr, D_RNN, D_MODEL), BETA * D_RNN ** -0.5),
        'w_ffn_in': nrm((N_DENSE_FFN, D_MODEL, 2 * D_FF), D_MODEL ** -0.5),
        'w_ffn_out': nrm((N_DENSE_FFN, D_FF, D_MODEL), BETA * D_FF ** -0.5),
        'w_router': nrm((N_MOE_FFN, D_MODEL, N_EXPERTS), D_MODEL ** -0.5),
        'w_moe_in': nrm((N_MOE_FFN, N_EXPERTS, D_MODEL, 2 * MOE_FF), D_MODEL ** -0.5),
        'w_moe_out': nrm((N_MOE_FFN, N_EXPERTS, MOE_FF, D_MODEL), BETA * MOE_FF ** -0.5),
    }


def reference(x_prompt, x_sample, cache_k, cache_v, state_conv, state_h, ln_g, ln_b,
              w_attn_in, rel_bias, w_attn_out, w_rnn_in, conv_w, conv_b, w_rg_a, b_rg_a,
              w_rg_x, b_rg_x, rg_lambda, w_rnn_out, w_ffn_in, w_ffn_out, w_router, w_moe_in, w_moe_out):
    bp, sp, _ = x_prompt.shape
    bs, ts, _ = x_sample.shape
    keep = min(ATT_REACH, sp)
    xp, xs = x_prompt, x_sample
    kp_l, vp_l, ks_l, vs_l = [], [], [], []
    cp_l, hp_l, cs_l, hs_l = [], [], [], []
    for layer in range(DEPTH):
        j = layer // 2
        if layer % 2 == 0:
            qp, kp, vp = qkv_heads(xp, w_attn_in[j])
            qs, ks, vs = qkv_heads(xs, w_attn_in[j])
            op = chunk_attention_prompt(qp, kp, vp, rel_bias[j])
            o_s = chunk_attention_sample(qs, ks, vs, cache_k[j], cache_v[j], rel_bias[j])
            mp = op.reshape(bp, sp, D_MODEL) @ w_attn_out[j]
            ms = o_s.reshape(bs, ts, D_MODEL) @ w_attn_out[j]
            kp_l.append(kp[:, sp - keep:])
            vp_l.append(vp[:, sp - keep:])
            ks_l.append(ks)
            vs_l.append(vs)
        else:
            conv0 = jnp.zeros((bp, CONV_W - 1, D_RNN), xp.dtype)
            h_init = jnp.zeros((bp, D_RNN), xp.dtype)
            mp, cp, hp = rglru_mixer(xp, conv0, h_init, w_rnn_in[j], conv_w[j], conv_b[j], w_rg_a[j], b_rg_a[j],
                                     w_rg_x[j], b_rg_x[j], rg_lambda[j], w_rnn_out[j])
            ms, cs, hs = rglru_mixer(xs, state_conv[j], state_h[j], w_rnn_in[j], conv_w[j], conv_b[j], w_rg_a[j],
                                     b_rg_a[j], w_rg_x[j], b_rg_x[j], rg_lambda[j], w_rnn_out[j])
            cp_l.append(cp)
            hp_l.append(hp)
            cs_l.append(cs)
            hs_l.append(hs)
        xp = layer_norm(ALPHA * xp + mp, ln_g[layer, 0], ln_b[layer, 0])
        xs = layer_norm(ALPHA * xs + ms, ln_g[layer, 0], ln_b[layer, 0])
        flat = jnp.concatenate([xp.reshape(-1, D_MODEL), xs.reshape(-1, D_MODEL)], axis=0)
        if layer % 2 == 0:
            f = swiglu(flat, w_ffn_in[j], w_ffn_out[j])
        else:
            f = moe_swiglu(flat, w_router[j], w_moe_in[j], w_moe_out[j])
        fp = f[:bp * sp].reshape(bp, sp, D_MODEL)
        fs = f[bp * sp:].reshape(bs, ts, D_MODEL)
        xp = layer_norm(ALPHA * xp + fp, ln_g[layer, 1], ln_b[layer, 1])
        xs = layer_norm(ALPHA * xs + fs, ln_g[layer, 1], ln_b[layer, 1])
    y_prompt, y_sample = xp, xs
    return (y_prompt, y_sample, jnp.stack(kp_l), jnp.stack(vp_l), jnp.stack(ks_l), jnp.stack(vs_l),
            jnp.stack(cp_l), jnp.stack(hp_l), jnp.stack(cs_l), jnp.stack(hs_l))
```

```python
import functools

import jax
import jax.numpy as jnp
from jax import lax
from jax.experimental import pallas as pl
from jax.experimental.pallas import tpu as pltpu

F32 = jnp.float32
BF16 = jnp.bfloat16

CHUNK = 64
N_LEFT_CHUNKS = 8
ATT_REACH = N_LEFT_CHUNKS * CHUNK
PAST_LEN = 1024
BAND = ATT_REACH + CHUNK
REL_CLIP = 256
HEAD_DIM = 64
CONV_W = 4
RG_C = 8.0
TOP_K = 2
LN_EPS = 1e-5
NEG_INF = -1e30

LANES = 128
SUBLANES = 8
VMEM_LIMIT = 56 * 1024 * 1024
MOE_TILE = 1024
FFN_OUT_CHUNK = 512


def _params(*sem):
    return pltpu.CompilerParams(dimension_semantics=sem, vmem_limit_bytes=VMEM_LIMIT)


def _layer_norm(z, g, b):
    mu = jnp.mean(z, axis=-1, keepdims=True)
    zc = z - mu
    var = jnp.mean(zc * zc, axis=-1, keepdims=True)
    return zc * lax.rsqrt(var + LN_EPS) * g + b


def _mm_body(x_ref, w_ref, o_ref):
    o_ref[...] = jnp.dot(x_ref[...].astype(BF16), w_ref[...],
                         preferred_element_type=F32).astype(o_ref.dtype)


def _matmul(x, w, out_dtype, tm=1024, tn=1024):
    m, k = x.shape
    n = w.shape[1]
    tm, tn = min(tm, m), min(tn, n)
    return pl.pallas_call(
        _mm_body,
        grid=(pl.cdiv(m, tm), n // tn),
        in_specs=[pl.BlockSpec((tm, k), lambda i, j: (i, 0)),
                  pl.BlockSpec((k, tn), lambda i, j: (0, j))],
        out_specs=pl.BlockSpec((tm, tn), lambda i, j: (i, j)),
        out_shape=jax.ShapeDtypeStruct((m, n), out_dtype),
        compiler_params=_params("parallel", "arbitrary"),
        name="matmul",
    )(x, w)


def _mm_res_ln_body(x_ref, w_ref, r_ref, g_ref, b_ref, o_ref, *, alpha):
    y = jnp.dot(x_ref[...], w_ref[...], preferred_element_type=F32)
    o_ref[...] = _layer_norm(alpha * r_ref[...] + y, g_ref[...], b_ref[...])


def _mm_res_ln(x, w, res, g, b, alpha, tm=512):
    m, k = x.shape
    d = w.shape[1]
    tm = min(tm, m)
    return pl.pallas_call(
        functools.partial(_mm_res_ln_body, alpha=alpha),
        grid=(pl.cdiv(m, tm),),
        in_specs=[pl.BlockSpec((tm, k), lambda i: (i, 0)),
                  pl.BlockSpec((k, d), lambda i: (0, 0)),
                  pl.BlockSpec((tm, d), lambda i: (i, 0)),
                  pl.BlockSpec((1, d), lambda i: (0, 0)),
                  pl.BlockSpec((1, d), lambda i: (0, 0))],
        out_specs=pl.BlockSpec((tm, d), lambda i: (i, 0)),
        out_shape=jax.ShapeDtypeStruct((m, d), F32),
        compiler_params=_params("parallel"),
        name="matmul_residual_layernorm",
    )(x, w, res, g.reshape(1, d), b.reshape(1, d))


def _swiglu_accumulate(f, x_ref, xb_ref, wg, wu, wo_ref, wo_lead, o_ref):
    @pl.when(f == 0)
    def _():
        xb_ref[...] = x_ref[...].astype(BF16)

    xb = xb_ref[...]
    hg = jnp.dot(xb, wg, preferred_element_type=F32)
    hu = jnp.dot(xb, wu, preferred_element_type=F32)
    h = (jax.nn.silu(hg) * hu).astype(BF16)
    d = o_ref.shape[1]
    tc = min(d, FFN_OUT_CHUNK)
    for c in range(d // tc):
        cs = slice(c * tc, (c + 1) * tc)
        part = jnp.dot(h, wo_ref[wo_lead + (slice(None), cs)], preferred_element_type=F32)

        @pl.when(f == 0)
        def _():
            o_ref[:, cs] = part

        @pl.when(f > 0)
        def _():
            o_ref[:, cs] += part


def _ffn_ln_body(x_ref, wg_ref, wu_ref, wo_ref, g_ref, b_ref, o_ref, xb_ref, *, alpha):
    f = pl.program_id(1)
    _swiglu_accumulate(f, x_ref, xb_ref, wg_ref[...], wu_ref[...], wo_ref, (), o_ref)

    @pl.when(f == pl.num_programs(1) - 1)
    def _():
        o_ref[...] = _layer_norm(alpha * x_ref[...] + o_ref[...], g_ref[...], b_ref[...])


def _ffn_ln(x, w_in, w_out, g, b, alpha, tm=1024, tf=512):
    m, d = x.shape
    ff = w_out.shape[0]
    tm, tf = min(tm, m), min(tf, ff)
    nf = ff // tf
    return pl.pallas_call(
        functools.partial(_ffn_ln_body, alpha=alpha),
        grid=(pl.cdiv(m, tm), nf),
        in_specs=[pl.BlockSpec((tm, d), lambda i, f: (i, 0), pipeline_mode=pl.Buffered(1)),
                  pl.BlockSpec((d, tf), lambda i, f: (0, f)),
                  pl.BlockSpec((d, tf), lambda i, f: (0, nf + f)),
                  pl.BlockSpec((tf, d), lambda i, f: (f, 0)),
                  pl.BlockSpec((1, d), lambda i, f: (0, 0)),
                  pl.BlockSpec((1, d), lambda i, f: (0, 0))],
        out_specs=pl.BlockSpec((tm, d), lambda i, f: (i, 0)),
        out_shape=jax.ShapeDtypeStruct((m, d), F32),
        scratch_shapes=[pltpu.VMEM((tm, d), BF16)],
        compiler_params=_params("parallel", "arbitrary"),
        name="swiglu_residual_layernorm",
    )(x, w_in, w_in, w_out, g.reshape(1, d), b.reshape(1, d))


def _bias_table_body(p_ref, o_ref):
    i = pl.program_id(0)
    width = p_ref.shape[1]
    shift = lax.rem(width - (CHUNK - 1) + i, width)
    o_ref[0] = pltpu.roll(p_ref[...], shift, axis=1)


def _bias_table(rel_bias):
    h = rel_bias.shape[0]
    width = 5 * LANES
    n_const = BAND - REL_CLIP
    n_var = width - n_const
    strip = jnp.concatenate(
        [jnp.broadcast_to(rel_bias[:, 2 * REL_CLIP:], (h, n_const)),
         jnp.flip(rel_bias[:, 2 * REL_CLIP - n_var:2 * REL_CLIP], axis=1)], axis=1)
    rolled = pl.pallas_call(
        _bias_table_body,
        grid=(CHUNK,),
        in_specs=[pl.BlockSpec((h, width), lambda i: (0, 0))],
        out_specs=pl.BlockSpec((1, h, width), lambda i: (i, 0, 0)),
        out_shape=jax.ShapeDtypeStruct((CHUNK, h, width), F32),
        compiler_params=_params("arbitrary"),
        name="attention_bias_table",
    )(strip)
    return jnp.transpose(rolled[:, :, :BAND], (1, 0, 2))


def _pair_attention(q2, kb, vb, t_a, t_b, valid):
    left = lax.broadcasted_iota(jnp.int32, q2.shape, 1) < HEAD_DIM
    outs = []
    for head, table in ((0, t_a), (1, t_b)):
        qm = jnp.where(left if head == 0 else jnp.logical_not(left), q2, jnp.zeros_like(q2))
        s = lax.dot_general(qm, kb, (((1,), (1,)), ((), ())), preferred_element_type=F32)
        s = s * (HEAD_DIM ** -0.5) + table
        s = jnp.where(valid, s, NEG_INF)
        m = jnp.max(s, axis=-1, keepdims=True)
        p = jnp.exp(s - m)
        p = p / jnp.sum(p, axis=-1, keepdims=True)
        outs.append(jnp.dot(p.astype(BF16), vb, preferred_element_type=F32))
    return jnp.where(left, outs[0], outs[1])


def _attn_prompt_body(q_ref, kp_ref, kc_ref, vp_ref, vc_ref, t_ref, o_ref, kbuf, vbuf, *, hp, tq):
    i = pl.program_id(2)
    kbuf[0:tq] = kp_ref[0]
    kbuf[tq:2 * tq] = kc_ref[0]
    vbuf[0:tq] = vp_ref[0]
    vbuf[tq:2 * tq] = vc_ref[0]
    col = lax.broadcasted_iota(jnp.int32, (CHUNK, BAND), 1)

    def chunk(j, carry):
        r0 = pl.multiple_of(j * CHUNK, CHUNK)
        valid = col + ((i - 1) * tq + j * CHUNK) >= 0
        for pp in range(hp):
            ls = slice(pp * LANES, (pp + 1) * LANES)
            o2 = _pair_attention(q_ref[0, pl.ds(r0, CHUNK), ls],
                                 kbuf[pl.ds(r0, BAND), ls], vbuf[pl.ds(r0, BAND), ls],
                                 t_ref[2 * pp], t_ref[2 * pp + 1], valid)
            o_ref[0, pl.ds(r0, CHUNK), ls] = o2.astype(o_ref.dtype)
        return carry

    lax.fori_loop(0, tq // CHUNK, chunk, 0)


def _attn_prompt(qkv, table, hp=2):
    b, s, d3 = qkv.shape
    d = d3 // 3
    w = hp * LANES
    nw = d // w
    tq = ATT_REACH
    return pl.pallas_call(
        functools.partial(_attn_prompt_body, hp=hp, tq=tq),
        grid=(b, nw, s // tq),
        in_specs=[pl.BlockSpec((1, tq, w), lambda bi, g, i: (bi, i, g)),
                  pl.BlockSpec((1, tq, w), lambda bi, g, i: (bi, jnp.maximum(i - 1, 0), nw + g)),
                  pl.BlockSpec((1, tq, w), lambda bi, g, i: (bi, i, nw + g)),
                  pl.BlockSpec((1, tq, w), lambda bi, g, i: (bi, jnp.maximum(i - 1, 0), 2 * nw + g)),
                  pl.BlockSpec((1, tq, w), lambda bi, g, i: (bi, i, 2 * nw + g)),
                  pl.BlockSpec((2 * hp, CHUNK, BAND), lambda bi, g, i: (g, 0, 0))],
        out_specs=pl.BlockSpec((1, tq, w), lambda bi, g, i: (bi, i, g)),
        out_shape=jax.ShapeDtypeStruct((b, s, d), BF16),
        scratch_shapes=[pltpu.VMEM((2 * tq, w), BF16), pltpu.VMEM((2 * tq, w), BF16)],
        compiler_params=_params("parallel", "parallel", "arbitrary"),
        name="band_attention_prompt",
    )(qkv, qkv, qkv, qkv, qkv, table)


def _attn_sample_body(q_ref, kn_ref, vn_ref, ck_ref, cv_ref, t_ref, o_ref, kbuf, vbuf, *, n_cache, t_new):
    d = q_ref.shape[2]
    n_keys = n_cache + t_new
    kbuf[0:n_cache] = ck_ref[0].astype(BF16)
    vbuf[0:n_cache] = cv_ref[0].astype(BF16)
    kbuf[n_cache:n_keys] = kn_ref[0]
    vbuf[n_cache:n_keys] = vn_ref[0]
    kbuf[n_keys:BAND] = jnp.zeros((BAND - n_keys, d), BF16)
    vbuf[n_keys:BAND] = jnp.zeros((BAND - n_keys, d), BF16)
    valid = lax.broadcasted_iota(jnp.int32, (t_new, BAND), 1) < n_keys
    for pp in range(d // LANES):
        ls = slice(pp * LANES, (pp + 1) * LANES)
        o2 = _pair_attention(q_ref[0, :, ls], kbuf[:, ls], vbuf[:, ls],
                             t_ref[2 * pp, 0:t_new], t_ref[2 * pp + 1, 0:t_new], valid)
        o_ref[0, :, ls] = o2.astype(o_ref.dtype)


def _attn_sample(qkv, cache_k, cache_v, table):
    b, t_new, d3 = qkv.shape
    d = d3 // 3
    n_cache = cache_k.shape[1]
    h = table.shape[0]
    assert n_cache == ATT_REACH and PAST_LEN % CHUNK == 0 and t_new <= CHUNK
    return pl.pallas_call(
        functools.partial(_attn_sample_body, n_cache=n_cache, t_new=t_new),
        grid=(b,),
        in_specs=[pl.BlockSpec((1, t_new, d), lambda bi: (bi, 0, 0)),
                  pl.BlockSpec((1, t_new, d), lambda bi: (bi, 0, 1)),
                  pl.BlockSpec((1, t_new, d), lambda bi: (bi, 0, 2)),
                  pl.BlockSpec((1, n_cache, d), lambda bi: (bi, 0, 0)),
                  pl.BlockSpec((1, n_cache, d), lambda bi: (bi, 0, 0)),
                  pl.BlockSpec((h, CHUNK, BAND), lambda bi: (0, 0, 0))],
        out_specs=pl.BlockSpec((1, t_new, d), lambda bi: (bi, 0, 0)),
        out_shape=jax.ShapeDtypeStruct((b, t_new, d), BF16),
        scratch_shapes=[pltpu.VMEM((BAND, d), BF16), pltpu.VMEM((BAND, d), BF16)],
        compiler_params=_params("parallel"),
        name="band_attention_sample",
    )(qkv, qkv, qkv, cache_k, cache_v, table)


def _rglru_body(u_ref, gate_ref, cs_ref, h0_ref, cw_ref, cb_ref, wa_ref, ba_ref, wx_ref, bx_ref, lam_ref,
                y_ref, co_ref, ho_ref, ubuf, a_s, b_s, hcar, *, tt, blk):
    t = pl.program_id(1)
    d = u_ref.shape[2]
    pad = SUBLANES
    tail = CONV_W - 1

    @pl.when(t == 0)
    def _():
        ubuf[pad - tail:pad] = cs_ref[0]
        hcar[...] = h0_ref[0]

    ubuf[pad:pad + tt] = u_ref[0]
    uc = cb_ref[...]
    for j in range(CONV_W):
        uc = uc + ubuf[pad - tail + j:pad - tail + j + tt] * cw_ref[j:j + 1]
    new_tail = ubuf[pad + tt - tail:pad + tt]
    ubuf[pad - tail:pad] = new_tail

    ucb = uc.astype(BF16)
    softplus_neg_lam = jnp.logaddexp(-lam_ref[...], 0.0)
    for n in range(d // blk):
        cs = slice(n * blk, (n + 1) * blk)
        r = jax.nn.sigmoid(jnp.dot(ucb[:, cs], wa_ref[n], preferred_element_type=F32) + ba_ref[:, cs])
        gi = jax.nn.sigmoid(jnp.dot(ucb[:, cs], wx_ref[n], preferred_element_type=F32) + bx_ref[:, cs])
        log_a = -RG_C * r * softplus_neg_lam[:, cs]
        a = jnp.exp(log_a)
        one_minus_a2 = -jnp.tanh(log_a) * (a * a + 1.0)
        a_s[:, cs] = a
        b_s[:, cs] = jnp.sqrt(one_minus_a2) * (gi * uc[:, cs])

    row = lax.broadcasted_iota(jnp.int32, (SUBLANES, d), 0)

    def group(gidx, h):
        r0 = pl.multiple_of(gidx * SUBLANES, SUBLANES)
        a8 = a_s[pl.ds(r0, SUBLANES), :]
        b8 = b_s[pl.ds(r0, SUBLANES), :]
        for sh in (1, 2, 4):
            keep = row >= sh
            a_prev = jnp.where(keep, pltpu.roll(a8, sh, axis=0), 1.0)
            b_prev = jnp.where(keep, pltpu.roll(b8, sh, axis=0), 0.0)
            b8 = a8 * b_prev + b8
            a8 = a8 * a_prev
        h8 = a8 * h + b8
        b_s[pl.ds(r0, SUBLANES), :] = h8
        return h8[SUBLANES - 1:SUBLANES, :]

    h_last = lax.fori_loop(0, tt // SUBLANES, group, hcar[...])
    hcar[...] = h_last
    y_ref[0] = (jax.nn.gelu(gate_ref[0]) * b_s[...]).astype(y_ref.dtype)

    @pl.when(t == pl.num_programs(1) - 1)
    def _():
        co_ref[0] = new_tail
        ho_ref[0] = h_last


def _rglru(ug, conv_state, h0, conv_w, conv_b, w_a, b_a, w_x, b_x, lam, tt=256):
    b, t_len, d2 = ug.shape
    d = d2 // 2
    nb, blk, _ = w_a.shape
    tt = min(tt, t_len)
    row = lambda v: v.reshape(1, d)
    y, conv_out, h_out = pl.pallas_call(
        functools.partial(_rglru_body, tt=tt, blk=blk),
        grid=(b, t_len // tt),
        in_specs=[pl.BlockSpec((1, tt, d), lambda bi, t: (bi, t, 0)),
                  pl.BlockSpec((1, tt, d), lambda bi, t: (bi, t, 1)),
                  pl.BlockSpec((1, CONV_W - 1, d), lambda bi, t: (bi, 0, 0)),
                  pl.BlockSpec((1, 1, d), lambda bi, t: (bi, 0, 0)),
                  pl.BlockSpec((CONV_W, d), lambda bi, t: (0, 0)),
                  pl.BlockSpec((1, d), lambda bi, t: (0, 0)),
                  pl.BlockSpec((nb, blk, blk), lambda bi, t: (0, 0, 0)),
                  pl.BlockSpec((1, d), lambda bi, t: (0, 0)),
                  pl.BlockSpec((nb, blk, blk), lambda bi, t: (0, 0, 0)),
                  pl.BlockSpec((1, d), lambda bi, t: (0, 0)),
                  pl.BlockSpec((1, d), lambda bi, t: (0, 0))],
        out_specs=[pl.BlockSpec((1, tt, d), lambda bi, t: (bi, t, 0)),
                   pl.BlockSpec((1, CONV_W - 1, d), lambda bi, t: (bi, 0, 0)),
                   pl.BlockSpec((1, 1, d), lambda bi, t: (bi, 0, 0))],
        out_shape=[jax.ShapeDtypeStruct((b, t_len, d), BF16),
                   jax.ShapeDtypeStruct((b, CONV_W - 1, d), F32),
                   jax.ShapeDtypeStruct((b, 1, d), F32)],
        scratch_shapes=[pltpu.VMEM((tt + SUBLANES, d), F32), pltpu.VMEM((tt, d), F32),
                        pltpu.VMEM((tt, d), F32), pltpu.VMEM((1, d), F32)],
        compiler_params=_params("arbitrary", "arbitrary"),
        name="rglru",
    )(ug, ug, conv_state, h0.reshape(b, 1, d), conv_w, row(conv_b), w_a.astype(BF16), row(b_a),
      w_x.astype(BF16), row(b_x), row(lam))
    return y, conv_out, h_out.reshape(b, d)


def _router_body(x_ref, wr_ref, e1_ref, e2_ref, g1_ref, g2_ref, p1_ref, p2_ref, cnt_ref, carry, *, tm):
    i = pl.program_id(0)
    n_exp = wr_ref.shape[0]

    @pl.when(i == 0)
    def _():
        carry[...] = jnp.zeros_like(carry)

    logits = lax.dot_general(wr_ref[...], x_ref[...], (((1,), (1,)), ((), ())),
                             precision=lax.Precision.HIGHEST, preferred_element_type=F32)
    row = lax.broadcasted_iota(jnp.int32, (n_exp, tm), 0)
    m1 = jnp.max(logits, axis=0, keepdims=True)
    i1 = jnp.min(jnp.where(logits == m1, row, n_exp), axis=0, keepdims=True)
    rest = jnp.where(row == i1, -jnp.inf, logits)
    m2 = jnp.max(rest, axis=0, keepdims=True)
    i2 = jnp.min(jnp.where(rest == m2, row, n_exp), axis=0, keepdims=True)
    z = jnp.exp(m2 - m1)
    den = 1.0 + z
    e1_ref[...] = i1
    e2_ref[...] = i2
    g1_ref[...] = 1.0 / den
    g2_ref[...] = z / den

    oh1 = row == i1
    oh2 = row == i2
    both = jnp.where(jnp.logical_or(oh1, oh2), 1.0, 0.0).astype(BF16)
    before = (lax.broadcasted_iota(jnp.int32, (tm, tm), 0) < lax.broadcasted_iota(jnp.int32, (tm, tm), 1))
    prefix = jnp.dot(both, jnp.where(before, 1.0, 0.0).astype(BF16), preferred_element_type=F32) + carry[...]
    p1_ref[...] = jnp.sum(jnp.where(oh1, prefix, 0.0), axis=0, keepdims=True).astype(jnp.int32)
    p2_ref[...] = jnp.sum(jnp.where(oh2, prefix, 0.0), axis=0, keepdims=True).astype(jnp.int32)
    carry[...] += jnp.sum(both.astype(F32), axis=1, keepdims=True)
    cnt_ref[...] = jnp.broadcast_to(carry[...], cnt_ref.shape).astype(jnp.int32)


def _router(x, w_router, tm=256):
    n, d = x.shape
    n_exp = w_router.shape[1]
    assert n % tm == 0
    vec = lambda dt: jax.ShapeDtypeStruct((1, n), dt)
    vspec = pl.BlockSpec((1, tm), lambda i: (0, i))
    e1, e2, g1, g2, p1, p2, cnt = pl.pallas_call(
        functools.partial(_router_body, tm=tm),
        grid=(n // tm,),
        in_specs=[pl.BlockSpec((tm, d), lambda i: (i, 0)),
                  pl.BlockSpec((n_exp, d), lambda i: (0, 0))],
        out_specs=[vspec] * 6 + [pl.BlockSpec((n_exp, LANES), lambda i: (0, 0))],
        out_shape=[vec(jnp.int32), vec(jnp.int32), vec(F32), vec(F32), vec(jnp.int32), vec(jnp.int32),
                   jax.ShapeDtypeStruct((n_exp, LANES), jnp.int32)],
        scratch_shapes=[pltpu.VMEM((n_exp, 1), F32)],
        compiler_params=_params("arbitrary"),
        name="moe_router",
    )(x, w_router.T)
    flat = lambda v: v.reshape(n)
    return flat(e1), flat(e2), flat(g1), flat(g2), flat(p1), flat(p2), cnt[:, 0]


def _dispatch_body(d1_ref, d2_ref, x_hbm, xs_in, xs_hbm, sem, *, tm):
    del xs_in
    base = pl.program_id(0) * tm

    def copies(r):
        src = x_hbm.at[pl.ds(base + r, 1)]
        return (pltpu.make_async_copy(src, xs_hbm.at[pl.ds(d1_ref[0, 0, r], 1)], sem),
                pltpu.make_async_copy(src, xs_hbm.at[pl.ds(d2_ref[0, 0, r], 1)], sem))

    def start(r, carry):
        for c in copies(r):
            c.start()
        return carry

    def wait(r, carry):
        for c in copies(r):
            c.wait()
        return carry

    lax.fori_loop(0, tm, start, 0)
    lax.fori_loop(0, tm, wait, 0)


def _dispatch(x, dest1, dest2, n_rows, tm=256):
    n, d = x.shape
    assert n % tm == 0
    ispec = pl.BlockSpec((1, 1, tm), lambda i: (i, 0, 0), memory_space=pltpu.SMEM)
    return pl.pallas_call(
        functools.partial(_dispatch_body, tm=tm),
        grid=(n // tm,),
        in_specs=[ispec, ispec, pl.BlockSpec(memory_space=pl.ANY), pl.BlockSpec(memory_space=pl.ANY)],
        out_specs=pl.BlockSpec(memory_space=pl.ANY),
        out_shape=jax.ShapeDtypeStruct((n_rows, d), F32),
        scratch_shapes=[pltpu.SemaphoreType.DMA(())],
        input_output_aliases={3: 0},
        compiler_params=_params("arbitrary"),
        name="moe_dispatch",
    )(dest1.reshape(n // tm, 1, tm), dest2.reshape(n // tm, 1, tm), x, jnp.zeros((n_rows, d), F32))


def _moe_ffn_body(be_ref, nu_ref, x_ref, wg_ref, wu_ref, wo_ref, o_ref, xb_ref):
    blk = pl.program_id(0)
    f = pl.program_id(1)

    @pl.when(blk < nu_ref[0])
    def _():
        _swiglu_accumulate(f, x_ref, xb_ref, wg_ref[0], wu_ref[0], wo_ref, (0,), o_ref)

    @pl.when(jnp.logical_and(blk >= nu_ref[0], f == 0))
    def _():
        o_ref[...] = jnp.zeros_like(o_ref)


def _moe_ffn(xs, block_e, n_used, w_in, w_out, tf=512):
    n_rows, d = xs.shape
    n_exp, ff, _ = w_out.shape
    tm = MOE_TILE
    tf = min(tf, ff)
    nf = ff // tf
    n_blocks = n_rows // tm

    def rows(blk, f, be, nu):
        return (jnp.minimum(blk, nu[0] - 1), 0)

    def fstep(blk, f, nu):
        return jnp.where(blk < nu[0], f, nf - 1)

    grid_spec = pltpu.PrefetchScalarGridSpec(
        num_scalar_prefetch=2,
        grid=(n_blocks, nf),
        in_specs=[pl.BlockSpec((tm, d), rows, pipeline_mode=pl.Buffered(1)),
                  pl.BlockSpec((1, d, tf), lambda blk, f, be, nu: (be[blk], 0, fstep(blk, f, nu))),
                  pl.BlockSpec((1, d, tf), lambda blk, f, be, nu: (be[blk], 0, nf + fstep(blk, f, nu))),
                  pl.BlockSpec((1, tf, d), lambda blk, f, be, nu: (be[blk], fstep(blk, f, nu), 0))],
        out_specs=pl.BlockSpec((tm, d), lambda blk, f, be, nu: (blk, 0)),
        scratch_shapes=[pltpu.VMEM((tm, d), BF16)],
    )
    return pl.pallas_call(
        _moe_ffn_body,
        grid_spec=grid_spec,
        out_shape=jax.ShapeDtypeStruct((n_rows, d), F32),
        compiler_params=_params("arbitrary", "arbitrary"),
        name="moe_experts",
    )(block_e, n_used, xs, w_in, w_in, w_out)


def _combine_body(d1_ref, d2_ref, x_ref, g1_ref, g2_ref, g_ref, b_ref, ys_hbm, o_ref, y1, y2, sem, *, tm, alpha):
    def copies(r):
        return (pltpu.make_async_copy(ys_hbm.at[pl.ds(d1_ref[0, 0, r], 1)], y1.at[pl.ds(r, 1)], sem),
                pltpu.make_async_copy(ys_hbm.at[pl.ds(d2_ref[0, 0, r], 1)], y2.at[pl.ds(r, 1)], sem))

    def start(r, carry):
        for c in copies(r):
            c.start()
        return carry

    def wait(r, carry):
        for c in copies(r):
            c.wait()
        return carry

    lax.fori_loop(0, tm, start, 0)
    lax.fori_loop(0, tm, wait, 0)
    mix = g1_ref[...] * y1[...] + g2_ref[...] * y2[...]
    o_ref[...] = _layer_norm(alpha * x_ref[...] + mix, g_ref[...], b_ref[...])


def _combine_ln(x, ys, dest1, dest2, gate1, gate2, g, b, alpha, tm=256):
    n, d = x.shape
    assert n % tm == 0
    ispec = pl.BlockSpec((1, 1, tm), lambda i: (i, 0, 0), memory_space=pltpu.SMEM)
    col = pl.BlockSpec((tm, 1), lambda i: (i, 0))
    return pl.pallas_call(
        functools.partial(_combine_body, tm=tm, alpha=alpha),
        grid=(n // tm,),
        in_specs=[ispec, ispec, pl.BlockSpec((tm, d), lambda i: (i, 0)), col, col,
                  pl.BlockSpec((1, d), lambda i: (0, 0)), pl.BlockSpec((1, d), lambda i: (0, 0)),
                  pl.BlockSpec(memory_space=pl.ANY)],
        out_specs=pl.BlockSpec((tm, d), lambda i: (i, 0)),
        out_shape=jax.ShapeDtypeStruct((n, d), F32),
        scratch_shapes=[pltpu.VMEM((tm, d), F32), pltpu.VMEM((tm, d), F32), pltpu.SemaphoreType.DMA(())],
        compiler_params=_params("arbitrary"),
        name="moe_combine_layernorm",
    )(dest1.reshape(n // tm, 1, tm), dest2.reshape(n // tm, 1, tm), x, gate1.reshape(n, 1), gate2.reshape(n, 1),
      g.reshape(1, d), b.reshape(1, d), ys)


def _moe_ln(x, w_router, w_in, w_out, g, b, alpha):
    n, d = x.shape
    n_exp = w_router.shape[1]
    tm = MOE_TILE
    e1, e2, g1, g2, p1, p2, counts = _router(x, w_router)
    padded = (counts + tm - 1) // tm * tm
    ends = jnp.cumsum(padded)
    starts = ends - padded
    dest1 = starts[e1] + p1
    dest2 = starts[e2] + p2
    n_blocks = -(-(n * TOP_K) // tm) + n_exp
    n_used = (ends[-1:] // tm).astype(jnp.int32)
    blk_start = jnp.minimum(jnp.arange(n_blocks, dtype=jnp.int32), n_used - 1) * tm
    block_e = jnp.minimum(jnp.searchsorted(ends, blk_start, side='right'), n_exp - 1).astype(jnp.int32)
    xs = _dispatch(x, dest1, dest2, n_blocks * tm)
    ys = _moe_ffn(xs, block_e, n_used, w_in, w_out)
    return _combine_ln(x, ys, dest1, dest2, g1, g2, g, b, alpha)


def kernel(x_prompt, x_sample, cache_k, cache_v, state_conv, state_h, ln_g, ln_b, w_attn_in, rel_bias, w_attn_out, w_rnn_in, conv_w, conv_b, w_rg_a, b_rg_a, w_rg_x, b_rg_x, rg_lambda, w_rnn_out, w_ffn_in, w_ffn_out, w_router, w_moe_in, w_moe_out):
    bp, sp, d = x_prompt.shape
    bs, ts, _ = x_sample.shape
    depth = ln_g.shape[0]
    alpha = (2.0 * depth) ** 0.25
    n_heads = rel_bias.shape[1]
    keep = min(ATT_REACH, sp)
    np_tok, ns_tok = bp * sp, bs * ts

    xp = x_prompt.reshape(np_tok, d)
    xs = x_sample.reshape(ns_tok, d)
    kp_l, vp_l, ks_l, vs_l, cp_l, hp_l, cs_l, hs_l = [], [], [], [], [], [], [], []
    for layer in range(depth):
        j = layer // 2
        g0, b0, g1, b1 = ln_g[layer, 0], ln_b[layer, 0], ln_g[layer, 1], ln_b[layer, 1]
        if layer % 2 == 0:
            w_in = w_attn_in[j].astype(BF16)
            w_out = w_attn_out[j].astype(BF16)
            table = _bias_table(rel_bias[j])
            qkv_p = _matmul(xp, w_in, BF16).reshape(bp, sp, 3 * d)
            qkv_s = _matmul(xs, w_in, BF16).reshape(bs, ts, 3 * d)
            x_keep = jnp.concatenate([xp.reshape(bp, sp, d)[:, sp - keep:].reshape(bp * keep, d), xs], axis=0)
            kv_keep = _matmul(x_keep, w_in[:, d:], F32)
            kv_p = kv_keep[:bp * keep].reshape(bp, keep, 2, n_heads, HEAD_DIM)
            kv_s = kv_keep[bp * keep:].reshape(bs, ts, 2, n_heads, HEAD_DIM)
            kp_l.append(kv_p[:, :, 0])
            vp_l.append(kv_p[:, :, 1])
            ks_l.append(kv_s[:, :, 0])
            vs_l.append(kv_s[:, :, 1])
            op = _attn_prompt(qkv_p, table)
            n_cache = cache_k.shape[2]
            o_s = _attn_sample(qkv_s, cache_k[j].reshape(bs, n_cache, d), cache_v[j].reshape(bs, n_cache, d), table)
            xp = _mm_res_ln(op.reshape(np_tok, d), w_out, xp, g0, b0, alpha)
            xs = _mm_res_ln(o_s.reshape(ns_tok, d), w_out, xs, g0, b0, alpha)
        else:
            w_in = w_rnn_in[j].astype(BF16)
            w_out = w_rnn_out[j].astype(BF16)
            rg = (conv_w[j], conv_b[j], w_rg_a[j], b_rg_a[j].reshape(-1), w_rg_x[j], b_rg_x[j].reshape(-1),
                  rg_lambda[j])
            ug_p = _matmul(xp, w_in, F32).reshape(bp, sp, 2 * d)
            ug_s = _matmul(xs, w_in, F32).reshape(bs, ts, 2 * d)
            yp, cp, hp = _rglru(ug_p, jnp.zeros((bp, CONV_W - 1, d), F32), jnp.zeros((bp, d), F32), *rg)
            ys, cs, hs = _rglru(ug_s, state_conv[j], state_h[j], *rg)
            cp_l.append(cp)
            hp_l.append(hp)
            cs_l.append(cs)
            hs_l.append(hs)
            xp = _mm_res_ln(yp.reshape(np_tok, d), w_out, xp, g0, b0, alpha)
            xs = _mm_res_ln(ys.reshape(ns_tok, d), w_out, xs, g0, b0, alpha)
        if layer % 2 == 0:
            w_in = w_ffn_in[j].astype(BF16)
            w_out = w_ffn_out[j].astype(BF16)
            xp = _ffn_ln(xp, w_in, w_out, g1, b1, alpha)
            xs = _ffn_ln(xs, w_in, w_out, g1, b1, alpha)
        else:
            flat = jnp.concatenate([xp, xs], axis=0)
            flat = _moe_ln(flat, w_router[j], w_moe_in[j].astype(BF16), w_moe_out[j].astype(BF16), g1, b1, alpha)
            xp, xs = flat[:np_tok], flat[np_tok:]
    return (xp.reshape(bp, sp, d), xs.reshape(bs, ts, d), jnp.stack(kp_l), jnp.stack(vp_l), jnp.stack(ks_l),
            jnp.stack(vs_l), jnp.stack(cp_l), jnp.stack(hp_l), jnp.stack(cs_l), jnp.stack(hs_l))
```

```python
import functools

import jax
import jax.numpy as jnp
from jax import lax
from jax.experimental import pallas as pl
from jax.experimental.pallas import tpu as pltpu

F32 = jnp.float32
BF16 = jnp.bfloat16

CHUNK = 64
N_LEFT_CHUNKS = 8
ATT_REACH = N_LEFT_CHUNKS * CHUNK
PAST_LEN = 1024
BAND = ATT_REACH + CHUNK
REL_CLIP = 256
HEAD_DIM = 64
CONV_W = 4
RG_C = 8.0
TOP_K = 2
LN_EPS = 1e-5
NEG_INF = -1e30

LANES = 128
SUBLANES = 8
VMEM_LIMIT = 56 * 1024 * 1024
MOE_TILE = 1024
FFN_OUT_CHUNK = 512


def _params(*sem):
    return pltpu.CompilerParams(dimension_semantics=sem, vmem_limit_bytes=VMEM_LIMIT)


def _layer_norm(z, g, b):
    mu = jnp.mean(z, axis=-1, keepdims=True)
    zc = z - mu
    var = jnp.mean(zc * zc, axis=-1, keepdims=True)
    return zc * lax.rsqrt(var + LN_EPS) * g + b


def _mm_body(x_ref, w_ref, o_ref):
    o_ref[...] = jnp.dot(x_ref[...].astype(BF16), w_ref[...],
                         preferred_element_type=F32).astype(o_ref.dtype)


def _matmul(x, w, out_dtype, tm=1024, tn=1024):
    m, k = x.shape
    n = w.shape[1]
    tm, tn = min(tm, m), min(tn, n)
    return pl.pallas_call(
        _mm_body,
        grid=(pl.cdiv(m, tm), n // tn),
        in_specs=[pl.BlockSpec((tm, k), lambda i, j: (i, 0)),
                  pl.BlockSpec((k, tn), lambda i, j: (0, j))],
        out_specs=pl.BlockSpec((tm, tn), lambda i, j: (i, j)),
        out_shape=jax.ShapeDtypeStruct((m, n), out_dtype),
        compiler_params=_params("parallel", "arbitrary"),
        name="matmul",
    )(x, w)


def _mm_res_ln_body(x_ref, w_ref, r_ref, g_ref, b_ref, o_ref, *, alpha):
    y = jnp.dot(x_ref[...], w_ref[...], preferred_element_type=F32)
    o_ref[...] = _layer_norm(alpha * r_ref[...] + y, g_ref[...], b_ref[...])


def _mm_res_ln(x, w, res, g, b, alpha, tm=512):
    m, k = x.shape
    d = w.shape[1]
    tm = min(tm, m)
    return pl.pallas_call(
        functools.partial(_mm_res_ln_body, alpha=alpha),
        grid=(pl.cdiv(m, tm),),
        in_specs=[pl.BlockSpec((tm, k), lambda i: (i, 0)),
                  pl.BlockSpec((k, d), lambda i: (0, 0)),
                  pl.BlockSpec((tm, d), lambda i: (i, 0)),
                  pl.BlockSpec((1, d), lambda i: (0, 0)),
                  pl.BlockSpec((1, d), lambda i: (0, 0))],
        out_specs=pl.BlockSpec((tm, d), lambda i: (i, 0)),
        out_shape=jax.ShapeDtypeStruct((m, d), F32),
        compiler_params=_params("parallel"),
        name="matmul_residual_layernorm",
    )(x, w, res, g.reshape(1, d), b.reshape(1, d))


def _swiglu_accumulate(f, x_ref, xb_ref, wg, wu, wo_ref, wo_lead, o_ref):
    @pl.when(f == 0)
    def _():
        xb_ref[...] = x_ref[...].astype(BF16)
        o_ref[...] = jnp.zeros_like(o_ref)

    xb = xb_ref[...]
    hg = jnp.dot(xb, wg, preferred_element_type=F32)
    hu = jnp.dot(xb, wu, preferred_element_type=F32)
    h = (jax.nn.silu(hg) * hu).astype(BF16)
    d = o_ref.shape[1]
    tc = min(d, FFN_OUT_CHUNK)
    for c in range(d // tc):
        cs = slice(c * tc, (c + 1) * tc)
        o_ref[:, cs] += jnp.dot(h, wo_ref[wo_lead + (slice(None), cs)], preferred_element_type=F32)


def _ffn_ln_body(x_ref, wg_ref, wu_ref, wo_ref, g_ref, b_ref, o_ref, xb_ref, *, alpha):
    f = pl.program_id(1)
    _swiglu_accumulate(f, x_ref, xb_ref, wg_ref[...], wu_ref[...], wo_ref, (), o_ref)

    @pl.when(f == pl.num_programs(1) - 1)
    def _():
        o_ref[...] = _layer_norm(alpha * x_ref[...] + o_ref[...], g_ref[...], b_ref[...])


def _ffn_ln(x, w_in, w_out, g, b, alpha, tm=1024, tf=512):
    m, d = x.shape
    ff = w_out.shape[0]
    tm, tf = min(tm, m), min(tf, ff)
    nf = ff // tf
    return pl.pallas_call(
        functools.partial(_ffn_ln_body, alpha=alpha),
        grid=(pl.cdiv(m, tm), nf),
        in_specs=[pl.BlockSpec((tm, d), lambda i, f: (i, 0), pipeline_mode=pl.Buffered(1)),
                  pl.BlockSpec((d, tf), lambda i, f: (0, f)),
                  pl.BlockSpec((d, tf), lambda i, f: (0, nf + f)),
                  pl.BlockSpec((tf, d), lambda i, f: (f, 0)),
                  pl.BlockSpec((1, d), lambda i, f: (0, 0)),
                  pl.BlockSpec((1, d), lambda i, f: (0, 0))],
        out_specs=pl.BlockSpec((tm, d), lambda i, f: (i, 0)),
        out_shape=jax.ShapeDtypeStruct((m, d), F32),
        scratch_shapes=[pltpu.VMEM((tm, d), BF16)],
        compiler_params=_params("parallel", "arbitrary"),
        name="swiglu_residual_layernorm",
    )(x, w_in, w_in, w_out, g.reshape(1, d), b.reshape(1, d))


def _bias_table_body(p_ref, o_ref):
    i = pl.program_id(0)
    width = p_ref.shape[1]
    shift = lax.rem(width - (CHUNK - 1) + i, width)
    o_ref[0] = pltpu.roll(p_ref[...], shift, axis=1)


def _bias_table(rel_bias):
    h = rel_bias.shape[0]
    width = 5 * LANES
    n_const = BAND - REL_CLIP
    n_var = width - n_const
    strip = jnp.concatenate(
        [jnp.broadcast_to(rel_bias[:, 2 * REL_CLIP:], (h, n_const)),
         jnp.flip(rel_bias[:, 2 * REL_CLIP - n_var:2 * REL_CLIP], axis=1)], axis=1)
    rolled = pl.pallas_call(
        _bias_table_body,
        grid=(CHUNK,),
        in_specs=[pl.BlockSpec((h, width), lambda i: (0, 0))],
        out_specs=pl.BlockSpec((1, h, width), lambda i: (i, 0, 0)),
        out_shape=jax.ShapeDtypeStruct((CHUNK, h, width), F32),
        compiler_params=_params("arbitrary"),
        name="attention_bias_table",
    )(strip)
    return jnp.transpose(rolled[:, :, :BAND], (1, 0, 2))


def _pair_attention(q2, kb, vb, t_a, t_b, valid):
    left = lax.broadcasted_iota(jnp.int32, q2.shape, 1) < HEAD_DIM
    outs = []
    for head, table in ((0, t_a), (1, t_b)):
        qm = jnp.where(left if head == 0 else jnp.logical_not(left), q2, jnp.zeros_like(q2))
        s = lax.dot_general(qm, kb, (((1,), (1,)), ((), ())), preferred_element_type=F32)
        s = s * (HEAD_DIM ** -0.5) + table
        s = jnp.where(valid, s, NEG_INF)
        m = jnp.max(s, axis=-1, keepdims=True)
        p = jnp.exp(s - m)
        p = p / jnp.sum(p, axis=-1, keepdims=True)
        outs.append(jnp.dot(p.astype(BF16), vb, preferred_element_type=F32))
    return jnp.where(left, outs[0], outs[1])


def _attn_prompt_body(q_ref, kp_ref, kc_ref, vp_ref, vc_ref, t_ref, o_ref, kbuf, vbuf, s_buf, tbl, *, hp, tq):
    i = pl.program_id(2)
    kbuf[0:tq] = kp_ref[0]
    kbuf[tq:2 * tq] = kc_ref[0]
    vbuf[0:tq] = vp_ref[0]
    vbuf[tq:2 * tq] = vc_ref[0]
    n_chunks = tq // CHUNK
    rows = 2 * CHUNK
    left = lax.broadcasted_iota(jnp.int32, (CHUNK, LANES), 1) < HEAD_DIM
    col = lax.broadcasted_iota(jnp.int32, (rows, BAND), 1)
    scale = jnp.asarray(HEAD_DIM ** -0.5, BF16)

    @pl.when(i == 0)
    def _():
        for pp in range(hp):
            table = t_ref[2 * pp:2 * pp + 2].reshape(rows, BAND)
            for j in range(n_chunks):
                tbl[pp, j * rows:(j + 1) * rows] = jnp.where(col >= ATT_REACH - j * CHUNK, table, NEG_INF)

    @pl.when(i == 1)
    def _():
        for pp in range(hp):
            table = t_ref[2 * pp:2 * pp + 2].reshape(rows, BAND)
            for j in range(n_chunks):
                tbl[pp, j * rows:(j + 1) * rows] = table

    for pp in range(hp):
        ls = slice(pp * LANES, (pp + 1) * LANES)
        for j in range(n_chunks):
            q2 = q_ref[0, j * CHUNK:(j + 1) * CHUNK, ls] * scale
            zero = jnp.zeros_like(q2)
            qm = jnp.concatenate([jnp.where(left, q2, zero), jnp.where(left, zero, q2)], axis=0)
            s = lax.dot_general(qm, kbuf[j * CHUNK:j * CHUNK + BAND, ls], (((1,), (1,)), ((), ())),
                                preferred_element_type=F32)
            s_buf[pp, j * rows:(j + 1) * rows] = s + tbl[pp, j * rows:(j + 1) * rows]

    for pp in range(hp):
        ls = slice(pp * LANES, (pp + 1) * LANES)
        for j in range(n_chunks):
            s = s_buf[pp, j * rows:(j + 1) * rows]
            p = jnp.exp(s - jnp.max(s, axis=-1, keepdims=True))
            p = p * (1.0 / jnp.sum(p, axis=-1, keepdims=True))
            o = jnp.dot(p.astype(BF16), vbuf[j * CHUNK:j * CHUNK + BAND, ls], preferred_element_type=F32)
            o_ref[0, j * CHUNK:(j + 1) * CHUNK, ls] = jnp.where(left, o[:CHUNK], o[CHUNK:]).astype(o_ref.dtype)


def _attn_prompt(qkv, table, hp=2):
    b, s, d3 = qkv.shape
    d = d3 // 3
    w = hp * LANES
    nw = d // w
    tq = ATT_REACH
    return pl.pallas_call(
        functools.partial(_attn_prompt_body, hp=hp, tq=tq),
        grid=(b, nw, s // tq),
        in_specs=[pl.BlockSpec((1, tq, w), lambda bi, g, i: (bi, i, g)),
                  pl.BlockSpec((1, tq, w), lambda bi, g, i: (bi, jnp.maximum(i - 1, 0), nw + g)),
                  pl.BlockSpec((1, tq, w), lambda bi, g, i: (bi, i, nw + g)),
                  pl.BlockSpec((1, tq, w), lambda bi, g, i: (bi, jnp.maximum(i - 1, 0), 2 * nw + g)),
                  pl.BlockSpec((1, tq, w), lambda bi, g, i: (bi, i, 2 * nw + g)),
                  pl.BlockSpec((2 * hp, CHUNK, BAND), lambda bi, g, i: (g, 0, 0))],
        out_specs=pl.BlockSpec((1, tq, w), lambda bi, g, i: (bi, i, g)),
        out_shape=jax.ShapeDtypeStruct((b, s, d), BF16),
        scratch_shapes=[pltpu.VMEM((2 * tq, w), BF16), pltpu.VMEM((2 * tq, w), BF16),
                        pltpu.VMEM((hp, 2 * tq, BAND), F32), pltpu.VMEM((hp, 2 * tq, BAND), F32)],
        compiler_params=_params("parallel", "parallel", "arbitrary"),
        name="band_attention_prompt",
    )(qkv, qkv, qkv, qkv, qkv, table)


def _attn_sample_body(q_ref, kn_ref, vn_ref, ck_ref, cv_ref, t_ref, o_ref, kbuf, vbuf, *, n_cache, t_new):
    d = q_ref.shape[2]
    n_keys = n_cache + t_new
    kbuf[0:n_cache] = ck_ref[0].astype(BF16)
    vbuf[0:n_cache] = cv_ref[0].astype(BF16)
    kbuf[n_cache:n_keys] = kn_ref[0]
    vbuf[n_cache:n_keys] = vn_ref[0]
    kbuf[n_keys:BAND] = jnp.zeros((BAND - n_keys, d), BF16)
    vbuf[n_keys:BAND] = jnp.zeros((BAND - n_keys, d), BF16)
    valid = lax.broadcasted_iota(jnp.int32, (t_new, BAND), 1) < n_keys
    for pp in range(d // LANES):
        ls = slice(pp * LANES, (pp + 1) * LANES)
        o2 = _pair_attention(q_ref[0, :, ls], kbuf[:, ls], vbuf[:, ls],
                             t_ref[2 * pp, 0:t_new], t_ref[2 * pp + 1, 0:t_new], valid)
        o_ref[0, :, ls] = o2.astype(o_ref.dtype)


def _attn_sample(qkv, cache_k, cache_v, table):
    b, t_new, d3 = qkv.shape
    d = d3 // 3
    n_cache = cache_k.shape[1]
    h = table.shape[0]
    assert n_cache == ATT_REACH and PAST_LEN % CHUNK == 0 and t_new <= CHUNK
    return pl.pallas_call(
        functools.partial(_attn_sample_body, n_cache=n_cache, t_new=t_new),
        grid=(b,),
        in_specs=[pl.BlockSpec((1, t_new, d), lambda bi: (bi, 0, 0)),
                  pl.BlockSpec((1, t_new, d), lambda bi: (bi, 0, 1)),
                  pl.BlockSpec((1, t_new, d), lambda bi: (bi, 0, 2)),
                  pl.BlockSpec((1, n_cache, d), lambda bi: (bi, 0, 0)),
                  pl.BlockSpec((1, n_cache, d), lambda bi: (bi, 0, 0)),
                  pl.BlockSpec((h, CHUNK, BAND), lambda bi: (0, 0, 0))],
        out_specs=pl.BlockSpec((1, t_new, d), lambda bi: (bi, 0, 0)),
        out_shape=jax.ShapeDtypeStruct((b, t_new, d), BF16),
        scratch_shapes=[pltpu.VMEM((BAND, d), BF16), pltpu.VMEM((BAND, d), BF16)],
        compiler_params=_params("parallel"),
        name="band_attention_sample",
    )(qkv, qkv, qkv, cache_k, cache_v, table)


def _rglru_body(u_ref, gate_ref, cs_ref, h0_ref, cw_ref, cb_ref, wa_ref, ba_ref, wx_ref, bx_ref, lam_ref,
                y_ref, co_ref, ho_ref, ubuf, a_s, b_s, hcar, *, tt, blk):
    t = pl.program_id(1)
    d = u_ref.shape[2]
    pad = SUBLANES
    tail = CONV_W - 1

    @pl.when(t == 0)
    def _():
        ubuf[pad - tail:pad] = cs_ref[0]
        hcar[...] = h0_ref[0]

    ubuf[pad:pad + tt] = u_ref[0]
    uc = cb_ref[...]
    for j in range(CONV_W):
        uc = uc + ubuf[pad - tail + j:pad - tail + j + tt] * cw_ref[j:j + 1]
    new_tail = ubuf[pad + tt - tail:pad + tt]
    ubuf[pad - tail:pad] = new_tail

    ucb = uc.astype(BF16)
    softplus_neg_lam = jnp.logaddexp(-lam_ref[...], 0.0)
    for n in range(d // blk):
        cs = slice(n * blk, (n + 1) * blk)
        r = jax.nn.sigmoid(jnp.dot(ucb[:, cs], wa_ref[n], preferred_element_type=F32) + ba_ref[:, cs])
        gi = jax.nn.sigmoid(jnp.dot(ucb[:, cs], wx_ref[n], preferred_element_type=F32) + bx_ref[:, cs])
        log_a = -RG_C * r * softplus_neg_lam[:, cs]
        a = jnp.exp(log_a)
        one_minus_a2 = -jnp.tanh(log_a) * (a * a + 1.0)
        a_s[:, cs] = a
        b_s[:, cs] = jnp.sqrt(one_minus_a2) * (gi * uc[:, cs])

    row = lax.broadcasted_iota(jnp.int32, (SUBLANES, d), 0)

    def group(gidx, h):
        r0 = pl.multiple_of(gidx * SUBLANES, SUBLANES)
        a8 = a_s[pl.ds(r0, SUBLANES), :]
        b8 = b_s[pl.ds(r0, SUBLANES), :]
        for sh in (1, 2, 4):
            keep = row >= sh
            a_prev = jnp.where(keep, pltpu.roll(a8, sh, axis=0), 1.0)
            b_prev = jnp.where(keep, pltpu.roll(b8, sh, axis=0), 0.0)
            b8 = a8 * b_prev + b8
            a8 = a8 * a_prev
        h8 = a8 * h + b8
        b_s[pl.ds(r0, SUBLANES), :] = h8
        return h8[SUBLANES - 1:SUBLANES, :]

    h_last = lax.fori_loop(0, tt // SUBLANES, group, hcar[...])
    hcar[...] = h_last
    y_ref[0] = (jax.nn.gelu(gate_ref[0]) * b_s[...]).astype(y_ref.dtype)

    @pl.when(t == pl.num_programs(1) - 1)
    def _():
        co_ref[0] = new_tail
        ho_ref[0] = h_last


def _rglru(ug, conv_state, h0, conv_w, conv_b, w_a, b_a, w_x, b_x, lam, tt=256):
    b, t_len, d2 = ug.shape
    d = d2 // 2
    nb, blk, _ = w_a.shape
    tt = min(tt, t_len)
    row = lambda v: v.reshape(1, d)
    y, conv_out, h_out = pl.pallas_call(
        functools.partial(_rglru_body, tt=tt, blk=blk),
        grid=(b, t_len // tt),
        in_specs=[pl.BlockSpec((1, tt, d), lambda bi, t: (bi, t, 0)),
                  pl.BlockSpec((1, tt, d), lambda bi, t: (bi, t, 1)),
                  pl.BlockSpec((1, CONV_W - 1, d), lambda bi, t: (bi, 0, 0)),
                  pl.BlockSpec((1, 1, d), lambda bi, t: (bi, 0, 0)),
                  pl.BlockSpec((CONV_W, d), lambda bi, t: (0, 0)),
                  pl.BlockSpec((1, d), lambda bi, t: (0, 0)),
                  pl.BlockSpec((nb, blk, blk), lambda bi, t: (0, 0, 0)),
                  pl.BlockSpec((1, d), lambda bi, t: (0, 0)),
                  pl.BlockSpec((nb, blk, blk), lambda bi, t: (0, 0, 0)),
                  pl.BlockSpec((1, d), lambda bi, t: (0, 0)),
                  pl.BlockSpec((1, d), lambda bi, t: (0, 0))],
        out_specs=[pl.BlockSpec((1, tt, d), lambda bi, t: (bi, t, 0)),
                   pl.BlockSpec((1, CONV_W - 1, d), lambda bi, t: (bi, 0, 0)),
                   pl.BlockSpec((1, 1, d), lambda bi, t: (bi, 0, 0))],
        out_shape=[jax.ShapeDtypeStruct((b, t_len, d), BF16),
                   jax.ShapeDtypeStruct((b, CONV_W - 1, d), F32),
                   jax.ShapeDtypeStruct((b, 1, d), F32)],
        scratch_shapes=[pltpu.VMEM((tt + SUBLANES, d), F32), pltpu.VMEM((tt, d), F32),
                        pltpu.VMEM((tt, d), F32), pltpu.VMEM((1, d), F32)],
        compiler_params=_params("arbitrary", "arbitrary"),
        name="rglru",
    )(ug, ug, conv_state, h0.reshape(b, 1, d), conv_w, row(conv_b), w_a.astype(BF16), row(b_a),
      w_x.astype(BF16), row(b_x), row(lam))
    return y, conv_out, h_out.reshape(b, d)


def _router_body(x_ref, wr_ref, e1_ref, e2_ref, g1_ref, g2_ref, p1_ref, p2_ref, cnt_ref, carry, *, tm):
    i = pl.program_id(0)
    n_exp = wr_ref.shape[0]

    @pl.when(i == 0)
    def _():
        carry[...] = jnp.zeros_like(carry)

    logits = lax.dot_general(wr_ref[...], x_ref[...], (((1,), (1,)), ((), ())),
                             precision=lax.Precision.HIGHEST, preferred_element_type=F32)
    row = lax.broadcasted_iota(jnp.int32, (n_exp, tm), 0)
    m1 = jnp.max(logits, axis=0, keepdims=True)
    i1 = jnp.min(jnp.where(logits == m1, row, n_exp), axis=0, keepdims=True)
    rest = jnp.where(row == i1, -jnp.inf, logits)
    m2 = jnp.max(rest, axis=0, keepdims=True)
    i2 = jnp.min(jnp.where(rest == m2, row, n_exp), axis=0, keepdims=True)
    z = jnp.exp(m2 - m1)
    den = 1.0 + z
    e1_ref[...] = i1
    e2_ref[...] = i2
    g1_ref[...] = 1.0 / den
    g2_ref[...] = z / den

    oh1 = row == i1
    oh2 = row == i2
    both = jnp.where(jnp.logical_or(oh1, oh2), 1.0, 0.0).astype(BF16)
    before = (lax.broadcasted_iota(jnp.int32, (tm, tm), 0) < lax.broadcasted_iota(jnp.int32, (tm, tm), 1))
    prefix = jnp.dot(both, jnp.where(before, 1.0, 0.0).astype(BF16), preferred_element_type=F32) + carry[...]
    p1_ref[...] = jnp.sum(jnp.where(oh1, prefix, 0.0), axis=0, keepdims=True).astype(jnp.int32)
    p2_ref[...] = jnp.sum(jnp.where(oh2, prefix, 0.0), axis=0, keepdims=True).astype(jnp.int32)
    carry[...] += jnp.sum(both.astype(F32), axis=1, keepdims=True)
    cnt_ref[...] = jnp.broadcast_to(carry[...], cnt_ref.shape).astype(jnp.int32)


def _router(x, w_router, tm=256):
    n, d = x.shape
    n_exp = w_router.shape[1]
    assert n % tm == 0
    vec = lambda dt: jax.ShapeDtypeStruct((1, n), dt)
    vspec = pl.BlockSpec((1, tm), lambda i: (0, i))
    e1, e2, g1, g2, p1, p2, cnt = pl.pallas_call(
        functools.partial(_router_body, tm=tm),
        grid=(n // tm,),
        in_specs=[pl.BlockSpec((tm, d), lambda i: (i, 0)),
                  pl.BlockSpec((n_exp, d), lambda i: (0, 0))],
        out_specs=[vspec] * 6 + [pl.BlockSpec((n_exp, LANES), lambda i: (0, 0))],
        out_shape=[vec(jnp.int32), vec(jnp.int32), vec(F32), vec(F32), vec(jnp.int32), vec(jnp.int32),
                   jax.ShapeDtypeStruct((n_exp, LANES), jnp.int32)],
        scratch_shapes=[pltpu.VMEM((n_exp, 1), F32)],
        compiler_params=_params("arbitrary"),
        name="moe_router",
    )(x, w_router.T)
    flat = lambda v: v.reshape(n)
    return flat(e1), flat(e2), flat(g1), flat(g2), flat(p1), flat(p2), cnt[:, 0]


def _dispatch_body(d1_ref, d2_ref, x_ref, xs_in, xs_hbm, sem, *, tm):
    del xs_in

    def copies(r):
        src = x_ref.at[pl.ds(r, 1)]
        return (pltpu.make_async_copy(src, xs_hbm.at[pl.ds(d1_ref[0, 0, r], 1)], sem),
                pltpu.make_async_copy(src, xs_hbm.at[pl.ds(d2_ref[0, 0, r], 1)], sem))

    def start(r, carry):
        for c in copies(r):
            c.start()
        return carry

    def wait(r, carry):
        for c in copies(r):
            c.wait()
        return carry

    lax.fori_loop(0, tm, start, 0)
    lax.fori_loop(0, tm, wait, 0)


def _dispatch(x, dest1, dest2, n_rows, tm=256):
    n, d = x.shape
    assert n % tm == 0
    ispec = pl.BlockSpec((1, 1, tm), lambda i: (i, 0, 0), memory_space=pltpu.SMEM)
    return pl.pallas_call(
        functools.partial(_dispatch_body, tm=tm),
        grid=(n // tm,),
        in_specs=[ispec, ispec, pl.BlockSpec((tm, d), lambda i: (i, 0)), pl.BlockSpec(memory_space=pl.ANY)],
        out_specs=pl.BlockSpec(memory_space=pl.ANY),
        out_shape=jax.ShapeDtypeStruct((n_rows, d), F32),
        scratch_shapes=[pltpu.SemaphoreType.DMA(())],
        input_output_aliases={3: 0},
        compiler_params=_params("arbitrary"),
        name="moe_dispatch",
    )(dest1.reshape(n // tm, 1, tm), dest2.reshape(n // tm, 1, tm), x, jnp.zeros((n_rows, d), F32))


def _moe_ffn_body(be_ref, nu_ref, x_ref, wg_ref, wu_ref, wo_ref, o_ref, xb_ref):
    blk = pl.program_id(0)
    f = pl.program_id(1)

    @pl.when(blk < nu_ref[0])
    def _():
        _swiglu_accumulate(f, x_ref, xb_ref, wg_ref[0], wu_ref[0], wo_ref, (0,), o_ref)

    @pl.when(jnp.logical_and(blk >= nu_ref[0], f == 0))
    def _():
        o_ref[...] = jnp.zeros_like(o_ref)


def _moe_ffn(xs, block_e, n_used, w_in, w_out, tf=512):
    n_rows, d = xs.shape
    n_exp, ff, _ = w_out.shape
    tm = MOE_TILE
    tf = min(tf, ff)
    nf = ff // tf
    n_blocks = n_rows // tm

    def rows(blk, f, be, nu):
        return (jnp.minimum(blk, nu[0] - 1), 0)

    def fstep(blk, f, nu):
        return jnp.where(blk < nu[0], f, nf - 1)

    grid_spec = pltpu.PrefetchScalarGridSpec(
        num_scalar_prefetch=2,
        grid=(n_blocks, nf),
        in_specs=[pl.BlockSpec((tm, d), rows, pipeline_mode=pl.Buffered(1)),
                  pl.BlockSpec((1, d, tf), lambda blk, f, be, nu: (be[blk], 0, fstep(blk, f, nu))),
                  pl.BlockSpec((1, d, tf), lambda blk, f, be, nu: (be[blk], 0, nf + fstep(blk, f, nu))),
                  pl.BlockSpec((1, tf, d), lambda blk, f, be, nu: (be[blk], fstep(blk, f, nu), 0))],
        out_specs=pl.BlockSpec((tm, d), lambda blk, f, be, nu: (blk, 0)),
        scratch_shapes=[pltpu.VMEM((tm, d), BF16)],
    )
    return pl.pallas_call(
        _moe_ffn_body,
        grid_spec=grid_spec,
        out_shape=jax.ShapeDtypeStruct((n_rows, d), F32),
        compiler_params=_params("arbitrary", "arbitrary"),
        name="moe_experts",
    )(block_e, n_used, xs, w_in, w_in, w_out)


def _combine_body(d1_ref, d2_ref, x_ref, g1_ref, g2_ref, g_ref, b_ref, ys_hbm, o_ref, y1, y2, sem, *, tm, alpha):
    def copies(r):
        return (pltpu.make_async_copy(ys_hbm.at[pl.ds(d1_ref[0, 0, r], 1)], y1.at[pl.ds(r, 1)], sem),
                pltpu.make_async_copy(ys_hbm.at[pl.ds(d2_ref[0, 0, r], 1)], y2.at[pl.ds(r, 1)], sem))

    def start(r, carry):
        for c in copies(r):
            c.start()
        return carry

    def wait(r, carry):
        for c in copies(r):
            c.wait()
        return carry

    lax.fori_loop(0, tm, start, 0)
    lax.fori_loop(0, tm, wait, 0)
    mix = g1_ref[...] * y1[...] + g2_ref[...] * y2[...]
    o_ref[...] = _layer_norm(alpha * x_ref[...] + mix, g_ref[...], b_ref[...])


def _combine_ln(x, ys, dest1, dest2, gate1, gate2, g, b, alpha, tm=256):
    n, d = x.shape
    assert n % tm == 0
    ispec = pl.BlockSpec((1, 1, tm), lambda i: (i, 0, 0), memory_space=pltpu.SMEM)
    col = pl.BlockSpec((tm, 1), lambda i: (i, 0))
    return pl.pallas_call(
        functools.partial(_combine_body, tm=tm, alpha=alpha),
        grid=(n // tm,),
        in_specs=[ispec, ispec, pl.BlockSpec((tm, d), lambda i: (i, 0)), col, col,
                  pl.BlockSpec((1, d), lambda i: (0, 0)), pl.BlockSpec((1, d), lambda i: (0, 0)),
                  pl.BlockSpec(memory_space=pl.ANY)],
        out_specs=pl.BlockSpec((tm, d), lambda i: (i, 0)),
        out_shape=jax.ShapeDtypeStruct((n, d), F32),
        scratch_shapes=[pltpu.VMEM((tm, d), F32), pltpu.VMEM((tm, d), F32), pltpu.SemaphoreType.DMA(())],
        compiler_params=_params("arbitrary"),
        name="moe_combine_layernorm",
    )(dest1.reshape(n // tm, 1, tm), dest2.reshape(n // tm, 1, tm), x, gate1.reshape(n, 1), gate2.reshape(n, 1),
      g.reshape(1, d), b.reshape(1, d), ys)


def _moe_ln(x, w_router, w_in, w_out, g, b, alpha):
    n, d = x.shape
    n_exp = w_router.shape[1]
    tm = MOE_TILE
    e1, e2, g1, g2, p1, p2, counts = _router(x, w_router)
    padded = (counts + tm - 1) // tm * tm
    ends = jnp.cumsum(padded)
    starts = ends - padded
    dest1 = starts[e1] + p1
    dest2 = starts[e2] + p2
    n_blocks = -(-(n * TOP_K) // tm) + n_exp
    n_used = (ends[-1:] // tm).astype(jnp.int32)
    blk_start = jnp.minimum(jnp.arange(n_blocks, dtype=jnp.int32), n_used - 1) * tm
    block_e = jnp.minimum(jnp.sum(ends[None, :] <= blk_start[:, None], axis=1), n_exp - 1).astype(jnp.int32)
    xs = _dispatch(x, dest1, dest2, n_blocks * tm)
    ys = _moe_ffn(xs, block_e, n_used, w_in, w_out)
    return _combine_ln(x, ys, dest1, dest2, g1, g2, g, b, alpha)


def kernel(x_prompt, x_sample, cache_k, cache_v, state_conv, state_h, ln_g, ln_b, w_attn_in, rel_bias, w_attn_out, w_rnn_in, conv_w, conv_b, w_rg_a, b_rg_a, w_rg_x, b_rg_x, rg_lambda, w_rnn_out, w_ffn_in, w_ffn_out, w_router, w_moe_in, w_moe_out):
    bp, sp, d = x_prompt.shape
    bs, ts, _ = x_sample.shape
    depth = ln_g.shape[0]
    alpha = (2.0 * depth) ** 0.25
    n_heads = rel_bias.shape[1]
    keep = min(ATT_REACH, sp)
    np_tok, ns_tok = bp * sp, bs * ts

    xp = x_prompt.reshape(np_tok, d)
    xs = x_sample.reshape(ns_tok, d)
    kp_l, vp_l, ks_l, vs_l, cp_l, hp_l, cs_l, hs_l = [], [], [], [], [], [], [], []
    for layer in range(depth):
        j = layer // 2
        g0, b0, g1, b1 = ln_g[layer, 0], ln_b[layer, 0], ln_g[layer, 1], ln_b[layer, 1]
        if layer % 2 == 0:
            w_in = w_attn_in[j].astype(BF16)
            w_out = w_attn_out[j].astype(BF16)
            table = _bias_table(rel_bias[j])
            qkv_p = _matmul(xp, w_in, BF16).reshape(bp, sp, 3 * d)
            qkv_s = _matmul(xs, w_in, BF16).reshape(bs, ts, 3 * d)
            x_keep = jnp.concatenate([xp.reshape(bp, sp, d)[:, sp - keep:].reshape(bp * keep, d), xs], axis=0)
            kv_keep = _matmul(x_keep, w_in[:, d:], F32)
            kv_p = kv_keep[:bp * keep].reshape(bp, keep, 2, n_heads, HEAD_DIM)
            kv_s = kv_keep[bp * keep:].reshape(bs, ts, 2, n_heads, HEAD_DIM)
            kp_l.append(kv_p[:, :, 0])
            vp_l.append(kv_p[:, :, 1])
            ks_l.append(kv_s[:, :, 0])
            vs_l.append(kv_s[:, :, 1])
            op = _attn_prompt(qkv_p, table)
            n_cache = cache_k.shape[2]
            o_s = _attn_sample(qkv_s, cache_k[j].reshape(bs, n_cache, d), cache_v[j].reshape(bs, n_cache, d), table)
            xp = _mm_res_ln(op.reshape(np_tok, d), w_out, xp, g0, b0, alpha)
            xs = _mm_res_ln(o_s.reshape(ns_tok, d), w_out, xs, g0, b0, alpha)
        else:
            w_in = w_rnn_in[j].astype(BF16)
            w_out = w_rnn_out[j].astype(BF16)
            rg = (conv_w[j], conv_b[j], w_rg_a[j], b_rg_a[j].reshape(-1), w_rg_x[j], b_rg_x[j].reshape(-1),
                  rg_lambda[j])
            ug_p = _matmul(xp, w_in, F32).reshape(bp, sp, 2 * d)
            ug_s = _matmul(xs, w_in, F32).reshape(bs, ts, 2 * d)
            yp, cp, hp = _rglru(ug_p, jnp.zeros((bp, CONV_W - 1, d), F32), jnp.zeros((bp, d), F32), *rg)
            ys, cs, hs = _rglru(ug_s, state_conv[j], state_h[j], *rg)
            cp_l.append(cp)
            hp_l.append(hp)
            cs_l.append(cs)
            hs_l.append(hs)
            xp = _mm_res_ln(yp.reshape(np_tok, d), w_out, xp, g0, b0, alpha)
            xs = _mm_res_ln(ys.reshape(ns_tok, d), w_out, xs, g0, b0, alpha)
        if layer % 2 == 0:
            w_in = w_ffn_in[j].astype(BF16)
            w_out = w_ffn_out[j].astype(BF16)
            xp = _ffn_ln(xp, w_in, w_out, g1, b1, alpha)
            xs = _ffn_ln(xs, w_in, w_out, g1, b1, alpha)
        else:
            flat = jnp.concatenate([xp, xs], axis=0)
            flat = _moe_ln(flat, w_router[j], w_moe_in[j].astype(BF16), w_moe_out[j].astype(BF16), g1, b1, alpha)
            xp, xs = flat[:np_tok], flat[np_tok:]
    return (xp.reshape(bp, sp, d), xs.reshape(bs, ts, d), jnp.stack(kp_l), jnp.stack(vp_l), jnp.stack(ks_l),
            jnp.stack(vs_l), jnp.stack(cp_l), jnp.stack(hp_l), jnp.stack(cs_l), jnp.stack(hs_l))
```

```python
import functools

import jax
import jax.numpy as jnp
from jax import lax
from jax.experimental import pallas as pl
from jax.experimental.pallas import tpu as pltpu

F32 = jnp.float32
BF16 = jnp.bfloat16

CHUNK = 64
N_LEFT_CHUNKS = 8
ATT_REACH = N_LEFT_CHUNKS * CHUNK
PAST_LEN = 1024
BAND = ATT_REACH + CHUNK
REL_CLIP = 256
HEAD_DIM = 64
CONV_W = 4
RG_C = 8.0
TOP_K = 2
LN_EPS = 1e-5
NEG_INF = -1e30

LANES = 128
SUBLANES = 8
VMEM_LIMIT = 56 * 1024 * 1024
MOE_TILE = 1024
FFN_OUT_CHUNK = 512
COMBINE_ROWS = 32


def _params(*sem):
    return pltpu.CompilerParams(dimension_semantics=sem, vmem_limit_bytes=VMEM_LIMIT)


def _layer_norm(z, g, b):
    mu = jnp.mean(z, axis=-1, keepdims=True)
    zc = z - mu
    var = jnp.mean(zc * zc, axis=-1, keepdims=True)
    return zc * lax.rsqrt(var + LN_EPS) * g + b


def _mm_body(x_ref, w_ref, o_ref):
    o_ref[...] = jnp.dot(x_ref[...].astype(BF16), w_ref[...],
                         preferred_element_type=F32).astype(o_ref.dtype)


def _matmul(x, w, out_dtype, tm=1024, tn=1024):
    m, k = x.shape
    n = w.shape[1]
    tm, tn = min(tm, m), min(tn, n)
    return pl.pallas_call(
        _mm_body,
        grid=(pl.cdiv(m, tm), n // tn),
        in_specs=[pl.BlockSpec((tm, k), lambda i, j: (i, 0)),
                  pl.BlockSpec((k, tn), lambda i, j: (0, j))],
        out_specs=pl.BlockSpec((tm, tn), lambda i, j: (i, j)),
        out_shape=jax.ShapeDtypeStruct((m, n), out_dtype),
        compiler_params=_params("parallel", "arbitrary"),
        name="matmul",
    )(x, w)


def _mm_res_ln_body(x_ref, w_ref, r_ref, g_ref, b_ref, o_ref, *, alpha):
    y = jnp.dot(x_ref[...], w_ref[...], preferred_element_type=F32)
    o_ref[...] = _layer_norm(alpha * r_ref[...] + y, g_ref[...], b_ref[...])


def _mm_res_ln(x, w, res, g, b, alpha, tm=512):
    m, k = x.shape
    d = w.shape[1]
    tm = min(tm, m)
    return pl.pallas_call(
        functools.partial(_mm_res_ln_body, alpha=alpha),
        grid=(pl.cdiv(m, tm),),
        in_specs=[pl.BlockSpec((tm, k), lambda i: (i, 0)),
                  pl.BlockSpec((k, d), lambda i: (0, 0)),
                  pl.BlockSpec((tm, d), lambda i: (i, 0)),
                  pl.BlockSpec((1, d), lambda i: (0, 0)),
                  pl.BlockSpec((1, d), lambda i: (0, 0))],
        out_specs=pl.BlockSpec((tm, d), lambda i: (i, 0)),
        out_shape=jax.ShapeDtypeStruct((m, d), F32),
        compiler_params=_params("parallel"),
        name="matmul_residual_layernorm",
    )(x, w, res, g.reshape(1, d), b.reshape(1, d))


def _swiglu_accumulate(f, x_ref, xb_ref, wg, wu, wo_ref, wo_lead, o_ref):
    @pl.when(f == 0)
    def _():
        xb_ref[...] = x_ref[...].astype(BF16)
        o_ref[...] = jnp.zeros_like(o_ref)

    xb = xb_ref[...]
    hg = jnp.dot(xb, wg, preferred_element_type=F32)
    hu = jnp.dot(xb, wu, preferred_element_type=F32)
    h = (jax.nn.silu(hg) * hu).astype(BF16)
    d = o_ref.shape[1]
    tc = min(d, FFN_OUT_CHUNK)
    for c in range(d // tc):
        cs = slice(c * tc, (c + 1) * tc)
        o_ref[:, cs] += jnp.dot(h, wo_ref[wo_lead + (slice(None), cs)], preferred_element_type=F32)


def _ffn_ln_body(x_ref, wg_ref, wu_ref, wo_ref, g_ref, b_ref, o_ref, xb_ref, *, alpha):
    f = pl.program_id(1)
    _swiglu_accumulate(f, x_ref, xb_ref, wg_ref[...], wu_ref[...], wo_ref, (), o_ref)

    @pl.when(f == pl.num_programs(1) - 1)
    def _():
        o_ref[...] = _layer_norm(alpha * x_ref[...] + o_ref[...], g_ref[...], b_ref[...])


def _ffn_ln(x, w_in, w_out, g, b, alpha, tm=1024, tf=512):
    m, d = x.shape
    ff = w_out.shape[0]
    tm, tf = min(tm, m), min(tf, ff)
    nf = ff // tf
    return pl.pallas_call(
        functools.partial(_ffn_ln_body, alpha=alpha),
        grid=(pl.cdiv(m, tm), nf),
        in_specs=[pl.BlockSpec((tm, d), lambda i, f: (i, 0), pipeline_mode=pl.Buffered(1)),
                  pl.BlockSpec((d, tf), lambda i, f: (0, f)),
                  pl.BlockSpec((d, tf), lambda i, f: (0, nf + f)),
                  pl.BlockSpec((tf, d), lambda i, f: (f, 0)),
                  pl.BlockSpec((1, d), lambda i, f: (0, 0)),
                  pl.BlockSpec((1, d), lambda i, f: (0, 0))],
        out_specs=pl.BlockSpec((tm, d), lambda i, f: (i, 0)),
        out_shape=jax.ShapeDtypeStruct((m, d), F32),
        scratch_shapes=[pltpu.VMEM((tm, d), BF16)],
        compiler_params=_params("parallel", "arbitrary"),
        name="swiglu_residual_layernorm",
    )(x, w_in, w_in, w_out, g.reshape(1, d), b.reshape(1, d))


def _bias_table_body(p_ref, o_ref):
    i = pl.program_id(0)
    width = p_ref.shape[1]
    shift = lax.rem(width - (CHUNK - 1) + i, width)
    o_ref[0] = pltpu.roll(p_ref[...], shift, axis=1)


def _bias_table(rel_bias):
    h = rel_bias.shape[0]
    width = 5 * LANES
    n_const = BAND - REL_CLIP
    n_var = width - n_const
    strip = jnp.concatenate(
        [jnp.broadcast_to(rel_bias[:, 2 * REL_CLIP:], (h, n_const)),
         jnp.flip(rel_bias[:, 2 * REL_CLIP - n_var:2 * REL_CLIP], axis=1)], axis=1)
    rolled = pl.pallas_call(
        _bias_table_body,
        grid=(CHUNK,),
        in_specs=[pl.BlockSpec((h, width), lambda i: (0, 0))],
        out_specs=pl.BlockSpec((1, h, width), lambda i: (i, 0, 0)),
        out_shape=jax.ShapeDtypeStruct((CHUNK, h, width), F32),
        compiler_params=_params("arbitrary"),
        name="attention_bias_table",
    )(strip)
    return jnp.transpose(rolled[:, :, :BAND], (1, 0, 2))


def _pair_attention(q2, kb, vb, t_a, t_b, valid):
    left = lax.broadcasted_iota(jnp.int32, q2.shape, 1) < HEAD_DIM
    outs = []
    for head, table in ((0, t_a), (1, t_b)):
        qm = jnp.where(left if head == 0 else jnp.logical_not(left), q2, jnp.zeros_like(q2))
        s = lax.dot_general(qm, kb, (((1,), (1,)), ((), ())), preferred_element_type=F32)
        s = s * (HEAD_DIM ** -0.5) + table
        s = jnp.where(valid, s, NEG_INF)
        m = jnp.max(s, axis=-1, keepdims=True)
        p = jnp.exp(s - m)
        p = p / jnp.sum(p, axis=-1, keepdims=True)
        outs.append(jnp.dot(p.astype(BF16), vb, preferred_element_type=F32))
    return jnp.where(left, outs[0], outs[1])


def _attn_prompt_body(q_ref, kp_ref, kc_ref, vp_ref, vc_ref, t_ref, o_ref, kbuf, vbuf, s_buf, tbl, *, hp, tq):
    i = pl.program_id(2)
    kbuf[0:tq] = kp_ref[0]
    kbuf[tq:2 * tq] = kc_ref[0]
    vbuf[0:tq] = vp_ref[0]
    vbuf[tq:2 * tq] = vc_ref[0]
    n_chunks = tq // CHUNK
    rows = 2 * CHUNK
    left = lax.broadcasted_iota(jnp.int32, (CHUNK, LANES), 1) < HEAD_DIM
    col = lax.broadcasted_iota(jnp.int32, (rows, BAND), 1)
    scale = jnp.asarray(HEAD_DIM ** -0.5, BF16)

    @pl.when(i == 0)
    def _():
        for pp in range(hp):
            table = t_ref[2 * pp:2 * pp + 2].reshape(rows, BAND)
            for j in range(n_chunks):
                tbl[pp, j * rows:(j + 1) * rows] = jnp.where(col >= ATT_REACH - j * CHUNK, table, NEG_INF)

    @pl.when(i == 1)
    def _():
        for pp in range(hp):
            table = t_ref[2 * pp:2 * pp + 2].reshape(rows, BAND)
            for j in range(n_chunks):
                tbl[pp, j * rows:(j + 1) * rows] = table

    for pp in range(hp):
        ls = slice(pp * LANES, (pp + 1) * LANES)
        for j in range(n_chunks):
            q2 = q_ref[0, j * CHUNK:(j + 1) * CHUNK, ls] * scale
            zero = jnp.zeros_like(q2)
            qm = jnp.concatenate([jnp.where(left, q2, zero), jnp.where(left, zero, q2)], axis=0)
            s = lax.dot_general(qm, kbuf[j * CHUNK:j * CHUNK + BAND, ls], (((1,), (1,)), ((), ())),
                                preferred_element_type=F32)
            s_buf[pp, j * rows:(j + 1) * rows] = s + tbl[pp, j * rows:(j + 1) * rows]

    for pp in range(hp):
        ls = slice(pp * LANES, (pp + 1) * LANES)
        for j in range(n_chunks):
            s = s_buf[pp, j * rows:(j + 1) * rows]
            p = jnp.exp(s - jnp.max(s, axis=-1, keepdims=True))
            p = p * (1.0 / jnp.sum(p, axis=-1, keepdims=True))
            o = jnp.dot(p.astype(BF16), vbuf[j * CHUNK:j * CHUNK + BAND, ls], preferred_element_type=F32)
            o_ref[0, j * CHUNK:(j + 1) * CHUNK, ls] = jnp.where(left, o[:CHUNK], o[CHUNK:]).astype(o_ref.dtype)


def _attn_prompt(qkv, table, hp=2):
    b, s, d3 = qkv.shape
    d = d3 // 3
    w = hp * LANES
    nw = d // w
    tq = ATT_REACH
    return pl.pallas_call(
        functools.partial(_attn_prompt_body, hp=hp, tq=tq),
        grid=(b, nw, s // tq),
        in_specs=[pl.BlockSpec((1, tq, w), lambda bi, g, i: (bi, i, g)),
                  pl.BlockSpec((1, tq, w), lambda bi, g, i: (bi, jnp.maximum(i - 1, 0), nw + g)),
                  pl.BlockSpec((1, tq, w), lambda bi, g, i: (bi, i, nw + g)),
                  pl.BlockSpec((1, tq, w), lambda bi, g, i: (bi, jnp.maximum(i - 1, 0), 2 * nw + g)),
                  pl.BlockSpec((1, tq, w), lambda bi, g, i: (bi, i, 2 * nw + g)),
                  pl.BlockSpec((2 * hp, CHUNK, BAND), lambda bi, g, i: (g, 0, 0))],
        out_specs=pl.BlockSpec((1, tq, w), lambda bi, g, i: (bi, i, g)),
        out_shape=jax.ShapeDtypeStruct((b, s, d), BF16),
        scratch_shapes=[pltpu.VMEM((2 * tq, w), BF16), pltpu.VMEM((2 * tq, w), BF16),
                        pltpu.VMEM((hp, 2 * tq, BAND), F32), pltpu.VMEM((hp, 2 * tq, BAND), F32)],
        compiler_params=_params("parallel", "parallel", "arbitrary"),
        name="band_attention_prompt",
    )(qkv, qkv, qkv, qkv, qkv, table)


def _attn_sample_body(q_ref, kn_ref, vn_ref, ck_ref, cv_ref, t_ref, o_ref, kbuf, vbuf, *, n_cache, t_new):
    d = q_ref.shape[2]
    n_keys = n_cache + t_new
    kbuf[0:n_cache] = ck_ref[0].astype(BF16)
    vbuf[0:n_cache] = cv_ref[0].astype(BF16)
    kbuf[n_cache:n_keys] = kn_ref[0]
    vbuf[n_cache:n_keys] = vn_ref[0]
    kbuf[n_keys:BAND] = jnp.zeros((BAND - n_keys, d), BF16)
    vbuf[n_keys:BAND] = jnp.zeros((BAND - n_keys, d), BF16)
    valid = lax.broadcasted_iota(jnp.int32, (t_new, BAND), 1) < n_keys
    for pp in range(d // LANES):
        ls = slice(pp * LANES, (pp + 1) * LANES)
        o2 = _pair_attention(q_ref[0, :, ls], kbuf[:, ls], vbuf[:, ls],
                             t_ref[2 * pp, 0:t_new], t_ref[2 * pp + 1, 0:t_new], valid)
        o_ref[0, :, ls] = o2.astype(o_ref.dtype)


def _attn_sample(qkv, cache_k, cache_v, table):
    b, t_new, d3 = qkv.shape
    d = d3 // 3
    n_cache = cache_k.shape[1]
    h = table.shape[0]
    assert n_cache == ATT_REACH and PAST_LEN % CHUNK == 0 and t_new <= CHUNK
    return pl.pallas_call(
        functools.partial(_attn_sample_body, n_cache=n_cache, t_new=t_new),
        grid=(b,),
        in_specs=[pl.BlockSpec((1, t_new, d), lambda bi: (bi, 0, 0)),
                  pl.BlockSpec((1, t_new, d), lambda bi: (bi, 0, 1)),
                  pl.BlockSpec((1, t_new, d), lambda bi: (bi, 0, 2)),
                  pl.BlockSpec((1, n_cache, d), lambda bi: (bi, 0, 0)),
                  pl.BlockSpec((1, n_cache, d), lambda bi: (bi, 0, 0)),
                  pl.BlockSpec((h, CHUNK, BAND), lambda bi: (0, 0, 0))],
        out_specs=pl.BlockSpec((1, t_new, d), lambda bi: (bi, 0, 0)),
        out_shape=jax.ShapeDtypeStruct((b, t_new, d), BF16),
        scratch_shapes=[pltpu.VMEM((BAND, d), BF16), pltpu.VMEM((BAND, d), BF16)],
        compiler_params=_params("parallel"),
        name="band_attention_sample",
    )(qkv, qkv, qkv, cache_k, cache_v, table)


def _rglru_body(u_ref, gate_ref, cs_ref, h0_ref, cw_ref, cb_ref, wa_ref, ba_ref, wx_ref, bx_ref, lam_ref,
                y_ref, co_ref, ho_ref, ubuf, a_s, b_s, hcar, *, tt, blk):
    t = pl.program_id(1)
    d = u_ref.shape[2]
    pad = SUBLANES
    tail = CONV_W - 1

    @pl.when(t == 0)
    def _():
        ubuf[pad - tail:pad] = cs_ref[0]
        hcar[...] = h0_ref[0]

    ubuf[pad:pad + tt] = u_ref[0]
    uc = cb_ref[...]
    for j in range(CONV_W):
        uc = uc + ubuf[pad - tail + j:pad - tail + j + tt] * cw_ref[j:j + 1]
    new_tail = ubuf[pad + tt - tail:pad + tt]
    ubuf[pad - tail:pad] = new_tail

    ucb = uc.astype(BF16)
    softplus_neg_lam = jnp.logaddexp(-lam_ref[...], 0.0)
    for n in range(d // blk):
        cs = slice(n * blk, (n + 1) * blk)
        r = jax.nn.sigmoid(jnp.dot(ucb[:, cs], wa_ref[n], preferred_element_type=F32) + ba_ref[:, cs])
        gi = jax.nn.sigmoid(jnp.dot(ucb[:, cs], wx_ref[n], preferred_element_type=F32) + bx_ref[:, cs])
        log_a = -RG_C * r * softplus_neg_lam[:, cs]
        a = jnp.exp(log_a)
        one_minus_a2 = -jnp.tanh(log_a) * (a * a + 1.0)
        a_s[:, cs] = a
        b_s[:, cs] = jnp.sqrt(one_minus_a2) * (gi * uc[:, cs])

    row = lax.broadcasted_iota(jnp.int32, (SUBLANES, d), 0)

    def group(gidx, h):
        r0 = pl.multiple_of(gidx * SUBLANES, SUBLANES)
        a8 = a_s[pl.ds(r0, SUBLANES), :]
        b8 = b_s[pl.ds(r0, SUBLANES), :]
        for sh in (1, 2, 4):
            keep = row >= sh
            a_prev = jnp.where(keep, pltpu.roll(a8, sh, axis=0), 1.0)
            b_prev = jnp.where(keep, pltpu.roll(b8, sh, axis=0), 0.0)
            b8 = a8 * b_prev + b8
            a8 = a8 * a_prev
        h8 = a8 * h + b8
        b_s[pl.ds(r0, SUBLANES), :] = h8
        return h8[SUBLANES - 1:SUBLANES, :]

    h_last = lax.fori_loop(0, tt // SUBLANES, group, hcar[...])
    hcar[...] = h_last
    y_ref[0] = (jax.nn.gelu(gate_ref[0]) * b_s[...]).astype(y_ref.dtype)

    @pl.when(t == pl.num_programs(1) - 1)
    def _():
        co_ref[0] = new_tail
        ho_ref[0] = h_last


def _rglru(ug, conv_state, h0, conv_w, conv_b, w_a, b_a, w_x, b_x, lam, tt=256):
    b, t_len, d2 = ug.shape
    d = d2 // 2
    nb, blk, _ = w_a.shape
    tt = min(tt, t_len)
    row = lambda v: v.reshape(1, d)
    y, conv_out, h_out = pl.pallas_call(
        functools.partial(_rglru_body, tt=tt, blk=blk),
        grid=(b, t_len // tt),
        in_specs=[pl.BlockSpec((1, tt, d), lambda bi, t: (bi, t, 0)),
                  pl.BlockSpec((1, tt, d), lambda bi, t: (bi, t, 1)),
                  pl.BlockSpec((1, CONV_W - 1, d), lambda bi, t: (bi, 0, 0)),
                  pl.BlockSpec((1, 1, d), lambda bi, t: (bi, 0, 0)),
                  pl.BlockSpec((CONV_W, d), lambda bi, t: (0, 0)),
                  pl.BlockSpec((1, d), lambda bi, t: (0, 0)),
                  pl.BlockSpec((nb, blk, blk), lambda bi, t: (0, 0, 0)),
                  pl.BlockSpec((1, d), lambda bi, t: (0, 0)),
                  pl.BlockSpec((nb, blk, blk), lambda bi, t: (0, 0, 0)),
                  pl.BlockSpec((1, d), lambda bi, t: (0, 0)),
                  pl.BlockSpec((1, d), lambda bi, t: (0, 0))],
        out_specs=[pl.BlockSpec((1, tt, d), lambda bi, t: (bi, t, 0)),
                   pl.BlockSpec((1, CONV_W - 1, d), lambda bi, t: (bi, 0, 0)),
                   pl.BlockSpec((1, 1, d), lambda bi, t: (bi, 0, 0))],
        out_shape=[jax.ShapeDtypeStruct((b, t_len, d), BF16),
                   jax.ShapeDtypeStruct((b, CONV_W - 1, d), F32),
                   jax.ShapeDtypeStruct((b, 1, d), F32)],
        scratch_shapes=[pltpu.VMEM((tt + SUBLANES, d), F32), pltpu.VMEM((tt, d), F32),
                        pltpu.VMEM((tt, d), F32), pltpu.VMEM((1, d), F32)],
        compiler_params=_params("arbitrary", "arbitrary"),
        name="rglru",
    )(ug, ug, conv_state, h0.reshape(b, 1, d), conv_w, row(conv_b), w_a.astype(BF16), row(b_a),
      w_x.astype(BF16), row(b_x), row(lam))
    return y, conv_out, h_out.reshape(b, d)


def _router_body(x_ref, wr_ref, c0_ref, e1_ref, e2_ref, g1_ref, g2_ref, p1_ref, p2_ref, cnt_ref, carry, *, tm):
    i = pl.program_id(0)
    n_exp = wr_ref.shape[0]

    @pl.when(i == 0)
    def _():
        carry[...] = c0_ref[:, 0:1].astype(F32)

    def split(v):
        hi = v.astype(BF16)
        return hi, (v - hi.astype(F32)).astype(BF16)

    def dot_nt(a, b):
        return lax.dot_general(a, b, (((1,), (1,)), ((), ())), preferred_element_type=F32)

    w_hi, w_lo = split(wr_ref[...])
    x_hi, x_lo = split(x_ref[...])
    logits = dot_nt(w_hi, x_hi) + (dot_nt(w_hi, x_lo) + dot_nt(w_lo, x_hi))
    row = lax.broadcasted_iota(jnp.int32, (n_exp, tm), 0)
    m1 = jnp.max(logits, axis=0, keepdims=True)
    i1 = jnp.min(jnp.where(logits == m1, row, n_exp), axis=0, keepdims=True)
    rest = jnp.where(row == i1, -jnp.inf, logits)
    m2 = jnp.max(rest, axis=0, keepdims=True)
    i2 = jnp.min(jnp.where(rest == m2, row, n_exp), axis=0, keepdims=True)
    z = jnp.exp(m2 - m1)
    den = 1.0 + z
    e1_ref[...] = i1
    e2_ref[...] = i2
    g1_ref[...] = 1.0 / den
    g2_ref[...] = z / den

    oh1 = row == i1
    oh2 = row == i2
    both = jnp.where(jnp.logical_or(oh1, oh2), 1.0, 0.0).astype(BF16)
    before = (lax.broadcasted_iota(jnp.int32, (tm, tm), 0) < lax.broadcasted_iota(jnp.int32, (tm, tm), 1))
    prefix = jnp.dot(both, jnp.where(before, 1.0, 0.0).astype(BF16), preferred_element_type=F32) + carry[...]
    p1_ref[...] = jnp.sum(jnp.where(oh1, prefix, 0.0), axis=0, keepdims=True).astype(jnp.int32)
    p2_ref[...] = jnp.sum(jnp.where(oh2, prefix, 0.0), axis=0, keepdims=True).astype(jnp.int32)
    carry[...] += jnp.sum(both.astype(F32), axis=1, keepdims=True)
    cnt_ref[...] = jnp.broadcast_to(carry[...], cnt_ref.shape).astype(jnp.int32)


def _router(x, w_router, counts_in, tm=256):
    n, d = x.shape
    n_exp = w_router.shape[1]
    assert n % tm == 0
    vec = lambda dt: jax.ShapeDtypeStruct((1, n), dt)
    vspec = pl.BlockSpec((1, tm), lambda i: (0, i))
    e1, e2, g1, g2, p1, p2, cnt = pl.pallas_call(
        functools.partial(_router_body, tm=tm),
        grid=(n // tm,),
        in_specs=[pl.BlockSpec((tm, d), lambda i: (i, 0)),
                  pl.BlockSpec((n_exp, d), lambda i: (0, 0)),
                  pl.BlockSpec((n_exp, LANES), lambda i: (0, 0))],
        out_specs=[vspec] * 6 + [pl.BlockSpec((n_exp, LANES), lambda i: (0, 0))],
        out_shape=[vec(jnp.int32), vec(jnp.int32), vec(F32), vec(F32), vec(jnp.int32), vec(jnp.int32),
                   jax.ShapeDtypeStruct((n_exp, LANES), jnp.int32)],
        scratch_shapes=[pltpu.VMEM((n_exp, 1), F32)],
        compiler_params=_params("arbitrary"),
        name="moe_router",
    )(x, w_router.T, counts_in)
    flat = lambda v: v.reshape(n)
    return flat(e1), flat(e2), flat(g1), flat(g2), flat(p1), flat(p2), cnt


def _dispatch_body(d1_ref, d2_ref, x_ref, xs_in, xs_hbm, sem, *, tm):
    del xs_in

    def copies(r):
        src = x_ref.at[pl.ds(r, 1)]
        return (pltpu.make_async_copy(src, xs_hbm.at[pl.ds(d1_ref[0, 0, r], 1)], sem),
                pltpu.make_async_copy(src, xs_hbm.at[pl.ds(d2_ref[0, 0, r], 1)], sem))

    for r in range(tm):
        for c in copies(r):
            c.start()
    for r in range(tm):
        for c in copies(r):
            c.wait()


def _dispatch(x, dest1, dest2, xs, tm=256):
    n, d = x.shape
    n_rows = xs.shape[0]
    assert n % tm == 0
    ispec = pl.BlockSpec((1, 1, tm), lambda i: (i, 0, 0), memory_space=pltpu.SMEM)
    return pl.pallas_call(
        functools.partial(_dispatch_body, tm=tm),
        grid=(n // tm,),
        in_specs=[ispec, ispec, pl.BlockSpec((tm, d), lambda i: (i, 0)), pl.BlockSpec(memory_space=pl.ANY)],
        out_specs=pl.BlockSpec(memory_space=pl.ANY),
        out_shape=jax.ShapeDtypeStruct((n_rows, d), F32),
        scratch_shapes=[pltpu.SemaphoreType.DMA(())],
        input_output_aliases={3: 0},
        compiler_params=_params("arbitrary"),
        name="moe_dispatch",
    )(dest1.reshape(n // tm, 1, tm), dest2.reshape(n // tm, 1, tm), x, xs)


def _moe_ffn_body(be_ref, nu_ref, x_ref, wg_ref, wu_ref, wo_ref, o_ref, xb_ref):
    blk = pl.program_id(0)
    f = pl.program_id(1)

    @pl.when(blk < nu_ref[0])
    def _():
        _swiglu_accumulate(f, x_ref, xb_ref, wg_ref[0], wu_ref[0], wo_ref, (0,), o_ref)

    @pl.when(jnp.logical_and(blk >= nu_ref[0], f == 0))
    def _():
        o_ref[...] = jnp.zeros_like(o_ref)


def _moe_ffn(xs, block_e, n_used, w_in, w_out, tf=512):
    n_rows, d = xs.shape
    n_exp, ff, _ = w_out.shape
    tm = MOE_TILE
    tf = min(tf, ff)
    nf = ff // tf
    n_blocks = n_rows // tm

    def rows(blk, f, be, nu):
        return (jnp.maximum(jnp.minimum(blk, nu[0] - 1), 0), 0)

    def fstep(blk, f, nu):
        return jnp.where(blk < nu[0], f, nf - 1)

    grid_spec = pltpu.PrefetchScalarGridSpec(
        num_scalar_prefetch=2,
        grid=(n_blocks, nf),
        in_specs=[pl.BlockSpec((tm, d), rows, pipeline_mode=pl.Buffered(1)),
                  pl.BlockSpec((1, d, tf), lambda blk, f, be, nu: (be[blk], 0, fstep(blk, f, nu))),
                  pl.BlockSpec((1, d, tf), lambda blk, f, be, nu: (be[blk], 0, nf + fstep(blk, f, nu))),
                  pl.BlockSpec((1, tf, d), lambda blk, f, be, nu: (be[blk], fstep(blk, f, nu), 0))],
        out_specs=pl.BlockSpec((tm, d), lambda blk, f, be, nu: (blk, 0)),
        scratch_shapes=[pltpu.VMEM((tm, d), BF16)],
    )
    return pl.pallas_call(
        _moe_ffn_body,
        grid_spec=grid_spec,
        out_shape=jax.ShapeDtypeStruct((n_rows, d), F32),
        compiler_params=_params("arbitrary", "arbitrary"),
        name="moe_experts",
    )(block_e, n_used, xs, w_in, w_in, w_out)


def _combine_body(d1_ref, d2_ref, n1_ref, n2_ref, x_ref, g1_ref, g2_ref, g_ref, b_ref, ys_hbm, o_ref, y1, y2, sem,
                  *, tm, alpha):
    i = pl.program_id(0)
    slot = lax.rem(i, 2)
    other = 1 - slot

    def copies(i1_ref, i2_ref, s, r):
        return (pltpu.make_async_copy(ys_hbm.at[pl.ds(i1_ref[0, 0, r], 1)], y1.at[s, pl.ds(r, 1)], sem.at[s]),
                pltpu.make_async_copy(ys_hbm.at[pl.ds(i2_ref[0, 0, r], 1)], y2.at[s, pl.ds(r, 1)], sem.at[s]))

    @pl.when(i == 0)
    def _():
        def start(r, carry):
            for c in copies(d1_ref, d2_ref, 0, r):
                c.start()
            return carry

        lax.fori_loop(0, tm, start, 0)

    for r in range(tm):
        for c in copies(d1_ref, d2_ref, slot, r):
            c.wait()
    for blk in range(tm // COMBINE_ROWS):
        rs = slice(blk * COMBINE_ROWS, (blk + 1) * COMBINE_ROWS)
        mix = g1_ref[rs] * y1[slot, rs] + g2_ref[rs] * y2[slot, rs]
        o_ref[rs] = _layer_norm(alpha * x_ref[rs] + mix, g_ref[...], b_ref[...])
        for r in range(rs.start, rs.stop):
            for c in copies(n1_ref, n2_ref, other, r):
                c.start()

    @pl.when(i == pl.num_programs(0) - 1)
    def _():
        def wait(r, carry):
            for c in copies(n1_ref, n2_ref, other, r):
                c.wait()
            return carry

        lax.fori_loop(0, tm, wait, 0)


def _combine_ln(x, ys, dest1, dest2, gate1, gate2, g, b, alpha, tm=256):
    n, d = x.shape
    assert n % tm == 0
    nt = n // tm
    ispec = pl.BlockSpec((1, 1, tm), lambda i: (i, 0, 0), memory_space=pltpu.SMEM)
    nspec = pl.BlockSpec((1, 1, tm), lambda i: (jnp.minimum(i + 1, nt - 1), 0, 0), memory_space=pltpu.SMEM)
    col = pl.BlockSpec((tm, 1), lambda i: (i, 0))
    dest1 = dest1.reshape(nt, 1, tm)
    dest2 = dest2.reshape(nt, 1, tm)
    return pl.pallas_call(
        functools.partial(_combine_body, tm=tm, alpha=alpha),
        grid=(nt,),
        in_specs=[ispec, ispec, nspec, nspec, pl.BlockSpec((tm, d), lambda i: (i, 0)), col, col,
                  pl.BlockSpec((1, d), lambda i: (0, 0)), pl.BlockSpec((1, d), lambda i: (0, 0)),
                  pl.BlockSpec(memory_space=pl.ANY)],
        out_specs=pl.BlockSpec((tm, d), lambda i: (i, 0)),
        out_shape=jax.ShapeDtypeStruct((n, d), F32),
        scratch_shapes=[pltpu.VMEM((2, tm, d), F32), pltpu.VMEM((2, tm, d), F32), pltpu.SemaphoreType.DMA((2,))],
        compiler_params=_params("arbitrary"),
        name="moe_combine_layernorm",
    )(dest1, dest2, dest1, dest2, x, gate1.reshape(n, 1), gate2.reshape(n, 1), g.reshape(1, d), b.reshape(1, d), ys)


def _moe_ln(xs_groups, w_router, w_in, w_out, g, b, alpha):
    d = xs_groups[0].shape[1]
    n_exp = w_router.shape[1]
    tm = MOE_TILE
    n_total = sum(x.shape[0] for x in xs_groups)
    counts = jnp.zeros((n_exp, LANES), jnp.int32)
    routed = []
    for x in xs_groups:
        *r, counts = _router(x, w_router, counts)
        routed.append(r)
    padded = (counts[:, 0] + tm - 1) // tm * tm
    ends = jnp.cumsum(padded)
    starts = ends - padded
    n_blocks = -(-(n_total * TOP_K) // tm) + n_exp
    n_used = (ends[-1:] // tm).astype(jnp.int32)
    blk_start = jnp.maximum(jnp.minimum(jnp.arange(n_blocks, dtype=jnp.int32), n_used - 1), 0) * tm
    block_e = jnp.minimum(jnp.sum(ends[None, :] <= blk_start[:, None], axis=1), n_exp - 1).astype(jnp.int32)
    dests = [(starts[e1] + p1, starts[e2] + p2) for e1, e2, _, _, p1, p2 in routed]
    rows = jnp.zeros((n_blocks * tm, d), F32)
    for x, (dest1, dest2) in zip(xs_groups, dests):
        rows = _dispatch(x, dest1, dest2, rows)
    ys = _moe_ffn(rows, block_e, n_used, w_in, w_out)
    return [_combine_ln(x, ys, dest1, dest2, r[2], r[3], g, b, alpha)
            for x, (dest1, dest2), r in zip(xs_groups, dests, routed)]


def kernel(x_prompt, x_sample, cache_k, cache_v, state_conv, state_h, ln_g, ln_b, w_attn_in, rel_bias, w_attn_out, w_rnn_in, conv_w, conv_b, w_rg_a, b_rg_a, w_rg_x, b_rg_x, rg_lambda, w_rnn_out, w_ffn_in, w_ffn_out, w_router, w_moe_in, w_moe_out):
    bp, sp, d = x_prompt.shape
    bs, ts, _ = x_sample.shape
    depth = ln_g.shape[0]
    alpha = (2.0 * depth) ** 0.25
    n_heads = rel_bias.shape[1]
    keep = min(ATT_REACH, sp)
    np_tok, ns_tok = bp * sp, bs * ts

    xp = x_prompt.reshape(np_tok, d)
    xs = x_sample.reshape(ns_tok, d)
    kp_l, vp_l, ks_l, vs_l, cp_l, hp_l, cs_l, hs_l = [], [], [], [], [], [], [], []
    for layer in range(depth):
        j = layer // 2
        g0, b0, g1, b1 = ln_g[layer, 0], ln_b[layer, 0], ln_g[layer, 1], ln_b[layer, 1]
        if layer % 2 == 0:
            w_in = w_attn_in[j].astype(BF16)
            w_out = w_attn_out[j].astype(BF16)
            table = _bias_table(rel_bias[j])
            qkv_p = _matmul(xp, w_in, BF16).reshape(bp, sp, 3 * d)
            qkv_s = _matmul(xs, w_in, BF16).reshape(bs, ts, 3 * d)
            x_keep = jnp.concatenate([xp.reshape(bp, sp, d)[:, sp - keep:].reshape(bp * keep, d), xs], axis=0)
            kv_keep = _matmul(x_keep, w_in[:, d:], F32)
            kv_p = kv_keep[:bp * keep].reshape(bp, keep, 2, n_heads, HEAD_DIM)
            kv_s = kv_keep[bp * keep:].reshape(bs, ts, 2, n_heads, HEAD_DIM)
            kp_l.append(kv_p[:, :, 0])
            vp_l.append(kv_p[:, :, 1])
            ks_l.append(kv_s[:, :, 0])
            vs_l.append(kv_s[:, :, 1])
            op = _attn_prompt(qkv_p, table)
            n_cache = cache_k.shape[2]
            o_s = _attn_sample(qkv_s, cache_k[j].reshape(bs, n_cache, d), cache_v[j].reshape(bs, n_cache, d), table)
            xp = _mm_res_ln(op.reshape(np_tok, d), w_out, xp, g0, b0, alpha)
            xs = _mm_res_ln(o_s.reshape(ns_tok, d), w_out, xs, g0, b0, alpha)
        else:
            w_in = w_rnn_in[j].astype(BF16)
            w_out = w_rnn_out[j].astype(BF16)
            rg = (conv_w[j], conv_b[j], w_rg_a[j], b_rg_a[j].reshape(-1), w_rg_x[j], b_rg_x[j].reshape(-1),
                  rg_lambda[j])
            ug_p = _matmul(xp, w_in, F32).reshape(bp, sp, 2 * d)
            ug_s = _matmul(xs, w_in, F32).reshape(bs, ts, 2 * d)
            yp, cp, hp = _rglru(ug_p, jnp.zeros((bp, CONV_W - 1, d), F32), jnp.zeros((bp, d), F32), *rg)
            ys, cs, hs = _rglru(ug_s, state_conv[j], state_h[j], *rg)
            cp_l.append(cp)
            hp_l.append(hp)
            cs_l.append(cs)
            hs_l.append(hs)
            xp = _mm_res_ln(yp.reshape(np_tok, d), w_out, xp, g0, b0, alpha)
            xs = _mm_res_ln(ys.reshape(ns_tok, d), w_out, xs, g0, b0, alpha)
        if layer % 2 == 0:
            w_in = w_ffn_in[j].astype(BF16)
            w_out = w_ffn_out[j].astype(BF16)
            xp = _ffn_ln(xp, w_in, w_out, g1, b1, alpha)
            xs = _ffn_ln(xs, w_in, w_out, g1, b1, alpha)
        else:
            xp, xs = _moe_ln([xp, xs], w_router[j], w_moe_in[j].astype(BF16), w_moe_out[j].astype(BF16), g1, b1,
                             alpha)
    return (xp.reshape(bp, sp, d), xs.reshape(bs, ts, d), jnp.stack(kp_l), jnp.stack(vp_l), jnp.stack(ks_l),
            jnp.stack(vs_l), jnp.stack(cp_l), jnp.stack(hp_l), jnp.stack(cs_l), jnp.stack(hs_l))
```
